```python
import math
import functools
import jax
import jax.numpy as jnp
from jax import lax
import numpy as np

D_MODEL = 2048
BATCH = 2
SEQ = 4096
DEPTH = 1
DEC_BATCH = 32
DEC_SEQ = 4
PAST_LEN = 16384
PAGE_SIZE = 128

N_META = 16
N_HEADS = 16
HEAD_DIM = D_MODEL // N_HEADS
ATT_W = N_HEADS * HEAD_DIM
Q_BLOCK = 128
D_INNER = 2 * D_MODEL
SSM_HEAD_DIM = 64
SSM_HEADS = D_INNER // SSM_HEAD_DIM
SSM_GROUPS = 8
D_STATE = 128
SSM_CONV = 4
SSM_CHUNK = 128
XBC_W = D_INNER + 2 * SSM_GROUPS * D_STATE
D_FF = 5632
FFN_CONV = 3
DN_ALPHA = (2.0 * DEPTH) ** 0.25
DN_BETA = (8.0 * DEPTH) ** -0.25
LN_EPS = 1e-5
RMS_EPS = 1e-5
IN_SIZES = (ATT_W, ATT_W, ATT_W, N_HEADS, D_INNER, XBC_W, SSM_HEADS, D_MODEL, D_MODEL)

kernel_name = "fox_mamba2_gated_hybrid_step"


def _split_in(h):
    idx = []
    acc = 0
    for s in IN_SIZES[:-1]:
        acc += s
        idx.append(acc)
    return jnp.split(h, idx, axis=-1)


def layer_norm(x, g, b):
    xf = x.astype(jnp.float32)
    mu = jnp.mean(xf, -1, keepdims=True)
    var = jnp.mean(jnp.square(xf - mu), -1, keepdims=True)
    return ((xf - mu) * lax.rsqrt(var + LN_EPS) * g + b).astype(x.dtype)


def gated_rms_norm(y, z, w):
    yf = (y * jax.nn.silu(z)).astype(jnp.float32)
    return (yf * lax.rsqrt(jnp.mean(yf * yf, -1, keepdims=True) + RMS_EPS) * w).astype(y.dtype)


def causal_dwconv(u, prev, w, b):
    K = w.shape[0]
    L = u.shape[1]
    full = jnp.concatenate([prev.astype(u.dtype), u], axis=1)
    out = b
    for j in range(K):
        out = out + full[:, j:j + L] * w[j]
    return out.astype(u.dtype), full[:, full.shape[1] - (K - 1):]


def fox_prompt(q, k, v, logf):
    n, L = q.shape[0], q.shape[1]
    nb = -(-L // Q_BLOCK)
    pad = nb * Q_BLOCK - L
    F = jnp.cumsum(logf, axis=1)
    Fk = F.transpose(0, 2, 1)
    qb = jnp.pad(q, ((0, 0), (0, pad), (0, 0), (0, 0))).reshape(n, nb, Q_BLOCK, N_HEADS, HEAD_DIM).swapaxes(0, 1)
    Fq = jnp.pad(F, ((0, 0), (0, pad), (0, 0))).reshape(n, nb, Q_BLOCK, N_HEADS).swapaxes(0, 1)
    kpos = jnp.arange(L)
    scale = HEAD_DIM ** -0.5

    def block(args):
        i, qi, Fi = args
        qpos = i * Q_BLOCK + jnp.arange(Q_BLOCK)
        s = jnp.einsum('bqhd,bkhd->bhqk', qi, k).astype(jnp.float32) * scale
        s = s + Fi.transpose(0, 2, 1)[:, :, :, None] - Fk[:, :, None, :]
        s = jnp.where((kpos[None, :] <= qpos[:, None])[None, None], s, -jnp.inf)
        p = jax.nn.softmax(s, axis=-1).astype(v.dtype)
        return jnp.einsum('bhqk,bkhd->bqhd', p, v)

    o = lax.map(block, (jnp.arange(nb), qb, Fq))
    return o.swapaxes(0, 1).reshape(n, nb * Q_BLOCK, N_HEADS, HEAD_DIM)[:, :L]


def fox_sample(q, k, v, logf, cache_k, cache_v, cache_logf, page_table, layer):
    T = q.shape[1]
    scale = HEAD_DIM ** -0.5
    causal = jnp.arange(T)[None, :] <= jnp.arange(T)[:, None]

    def one(args):
        qs, ks, vs, lfs, pages = args
        kp = cache_k[layer, pages].reshape(-1, N_HEADS, HEAD_DIM)
        vp = cache_v[layer, pages].reshape(-1, N_HEADS, HEAD_DIM)
        lp = cache_logf[layer, pages].reshape(-1, N_HEADS).astype(jnp.float32)
        suffix = lax.cumsum(lp, axis=0, reverse=True) - lp
        cnew = jnp.cumsum(lfs, axis=0)
        s_past = (jnp.einsum('thd,shd->hts', qs, kp).astype(jnp.float32) * scale
                  + suffix.T[:, None, :] + cnew.T[:, :, None])
        s_new = (jnp.einsum('thd,shd->hts', qs, ks).astype(jnp.float32) * scale
                 + cnew.T[:, :, None] - cnew.T[:, None, :])
        s_new = jnp.where(causal[None], s_new, -jnp.inf)
        p = jax.nn.softmax(jnp.concatenate([s_past, s_new], axis=-1), axis=-1).astype(vs.dtype)
        P = kp.shape[0]
        return (jnp.einsum('hts,shd->thd', p[..., :P], vp)
                + jnp.einsum('hts,shd->thd', p[..., P:], vs))

    return lax.map(one, (q, k, v, logf, page_table))


def ssd_chunked(x, dt, A, B, C, h0, chunk):
    n, L = x.shape[0], x.shape[1]
    nc = L // chunk
    G = SSM_GROUPS
    E = SSM_HEADS // G
    xc = x.reshape(n, nc, chunk, G, E, SSM_HEAD_DIM)
    dtc = dt.reshape(n, nc, chunk, G, E)
    Bc = B.reshape(n, nc, chunk, G, D_STATE)
    Cc = C.reshape(n, nc, chunk, G, D_STATE)
    cum = jnp.cumsum(dtc * A.reshape(G, E), axis=2)
    seg = cum[:, :, :, None] - cum[:, :, None, :]
    mask = jnp.tril(jnp.ones((chunk, chunk), bool))[:, :, None, None]
    decay = jnp.exp(jnp.where(mask, seg, -jnp.inf))
    CB = jnp.einsum('nclgk,ncsgk->nclsg', Cc, Bc)
    w = CB[..., None] * decay * dtc[:, :, None]
    y_diag = jnp.einsum('nclsge,ncsgep->nclgep', w, xc)
    decay_end = jnp.exp(cum[:, :, -1:] - cum)
    states = jnp.einsum('nclgk,nclge,nclgep->ncgepk', Bc, decay_end * dtc, xc)
    chunk_decay = jnp.exp(cum[:, :, -1])

    def step(h, inp):
        st, dec = inp
        return h * dec[..., None, None] + st, h

    h_init = h0.reshape(n, G, E, SSM_HEAD_DIM, D_STATE).astype(jnp.float32)
    hT, h_in = lax.scan(step, h_init, (states.swapaxes(0, 1), chunk_decay.swapaxes(0, 1)))
    h_in = h_in.swapaxes(0, 1)
    y_off = jnp.einsum('nclgk,ncgepk,nclge->nclgep', Cc, h_in, jnp.exp(cum))
    y = (y_diag + y_off).reshape(n, L, SSM_HEADS, SSM_HEAD_DIM)
    return y, hT.reshape(n, SSM_HEADS, SSM_HEAD_DIM, D_STATE).astype(h0.dtype)


def ssm_mix(z, xbc, dt_raw, conv_prev, h0, segments, conv_w, conv_b, dt_bias, a_log, d_skip, norm_w):
    n, L = z.shape[0], z.shape[1]
    xbc_c, conv_new = causal_dwconv(xbc, conv_prev, conv_w, conv_b)
    xbc_c = jax.nn.silu(xbc_c)
    xs, Bm, Cm = jnp.split(xbc_c, [D_INNER, D_INNER + SSM_GROUPS * D_STATE], axis=-1)
    xs = xs.reshape(n, L, SSM_HEADS, SSM_HEAD_DIM)
    Bm = Bm.reshape(n, L, SSM_GROUPS, D_STATE)
    Cm = Cm.reshape(n, L, SSM_GROUPS, D_STATE)
    dt = jax.nn.softplus(dt_raw.astype(jnp.float32) + dt_bias)
    A = -jnp.exp(a_log.astype(jnp.float32))
    h = h0
    ys = []
    start = 0
    for length, chunk in segments:
        sl = slice(start, start + length)
        y, h = ssd_chunked(xs[:, sl], dt[:, sl], A, Bm[:, sl], Cm[:, sl], h, chunk)
        ys.append(y)
        start += length
    y = jnp.concatenate(ys, axis=1)
    y = (y + xs * d_skip[:, None]).reshape(n, L, D_INNER).astype(z.dtype)
    return gated_rms_norm(y, z, norm_w), conv_new, h


def conv_ffn(x, prev, w_up, conv_w, conv_b, w_down):
    u, new_prev = causal_dwconv(x @ w_up, prev, conv_w, conv_b)
    a, b = jnp.split(u, 2, axis=-1)
    return (jax.nn.silu(a) * b) @ w_down, new_prev


def _layer(x, attn_fn, conv_ssm_prev, h_prev, conv_ffn_prev, segments, lw):
    (w_in, b_f, w_att_out, ssm_conv_w, ssm_conv_b, dt_bias, a_log, d_skip, ssm_norm_w,
     w_ssm_out, w_o, ln1_g, ln1_b, w_up, ffn_conv_w, ffn_conv_b, w_down, ln2_g, ln2_b) = lw
    n, L = x.shape[0], x.shape[1]
    q, k, v, f_logit, z, xbc, dt_raw, g_att, g_ssm = _split_in(x @ w_in)
    q = q.reshape(n, L, N_HEADS, HEAD_DIM)
    k = k.reshape(n, L, N_HEADS, HEAD_DIM)
    v = v.reshape(n, L, N_HEADS, HEAD_DIM)
    logf = jax.nn.log_sigmoid(f_logit.astype(jnp.float32) + b_f)
    o_att = attn_fn(q, k, v, logf).reshape(n, L, ATT_W) @ w_att_out
    y_ssm, conv_new, h_new = ssm_mix(z, xbc, dt_raw, conv_ssm_prev, h_prev, segments,
                                     ssm_conv_w, ssm_conv_b, dt_bias, a_log, d_skip, ssm_norm_w)
    o_ssm = y_ssm @ w_ssm_out
    merged = jax.nn.sigmoid(g_att) * o_att + jax.nn.sigmoid(g_ssm) * o_ssm
    x = layer_norm(DN_ALPHA * x + merged @ w_o, ln1_g, ln1_b)
    f_out, ffn_new = conv_ffn(x, conv_ffn_prev, w_up, ffn_conv_w, ffn_conv_b, w_down)
    x = layer_norm(DN_ALPHA * x + f_out, ln2_g, ln2_b)
    return x, (k, v, logf.astype(x.dtype), conv_new, h_new, ffn_new)


def setup_inputs(seed: int = 0) -> dict:
    key = jax.random.key(seed)
    ks = jax.random.split(key, 32)
    f32 = jnp.float32
    n_pages = PAST_LEN // PAGE_SIZE
    n_used = DEC_BATCH * n_pages
    n_pool = (5 * n_used + 3) // 4

    def nrm(i, shape, s=1.0):
        return s * jax.random.normal(ks[i], shape, f32)

    def gain(i, shape):
        return 1.0 + 0.1 * jax.random.normal(ks[i], shape, f32)

    dt0 = jnp.exp(jax.random.uniform(ks[17], (DEPTH, SSM_HEADS), f32, math.log(1e-3), math.log(1e-1)))
    cache_head_bias = jax.random.uniform(ks[31], (DEPTH, 1, 1, N_HEADS), f32, 2.0, 10.0)
    return {
        "x_prompt": nrm(0, (BATCH, SEQ, D_MODEL)),
        "x_sample": nrm(1, (DEC_BATCH, DEC_SEQ, D_MODEL)),
        "cache_k": nrm(2, (DEPTH, n_pool, PAGE_SIZE, N_HEADS, HEAD_DIM)),
        "cache_v": nrm(3, (DEPTH, n_pool, PAGE_SIZE, N_HEADS, HEAD_DIM)),
        "cache_logf": jax.nn.log_sigmoid(cache_head_bias + nrm(4, (DEPTH, n_pool, PAGE_SIZE, N_HEADS), 0.5)),
        "state_conv_ssm": nrm(5, (DEPTH, DEC_BATCH, SSM_CONV - 1, XBC_W)),
        "state_ssm": nrm(6, (DEPTH, DEC_BATCH, SSM_HEADS, SSM_HEAD_DIM, D_STATE), 0.1),
        "state_conv_ffn": nrm(7, (DEPTH, DEC_BATCH, FFN_CONV - 1, 2 * D_FF)),
        "page_table": jax.random.permutation(ks[8], n_pool)[:n_used].reshape(DEC_BATCH, n_pages).astype(jnp.int32),
        "meta_tokens": nrm(9, (N_META, D_MODEL)),
        "ln_in_g": gain(10, (D_MODEL,)),
        "ln_in_b": nrm(11, (D_MODEL,), 0.01),
        "w_in": nrm(12, (DEPTH, D_MODEL, sum(IN_SIZES)), D_MODEL ** -0.5),
        "b_f": jax.random.uniform(ks[13], (DEPTH, N_HEADS), f32, 2.0, 10.0),
        "w_att_out": nrm(14, (DEPTH, ATT_W, D_MODEL), DN_BETA * ATT_W ** -0.5),
        "ssm_conv_w": nrm(15, (DEPTH, SSM_CONV, XBC_W), SSM_CONV ** -0.5),
        "ssm_conv_b": nrm(16, (DEPTH, XBC_W), 0.01),
        "dt_bias": dt0 + jnp.log(-jnp.expm1(-dt0)),
        "a_log": jnp.log(jax.random.uniform(ks[18], (DEPTH, SSM_HEADS), f32, 1.0, 16.0)),
        "d_skip": gain(19, (DEPTH, SSM_HEADS)),
        "ssm_norm_w": gain(20, (DEPTH, D_INNER)),
        "w_ssm_out": nrm(21, (DEPTH, D_INNER, D_MODEL), DN_BETA * D_INNER ** -0.5),
        "w_o": nrm(22, (DEPTH, D_MODEL, D_MODEL), DN_BETA * D_MODEL ** -0.5),
        "ln1_g": gain(23, (DEPTH, D_MODEL)),
        "ln1_b": nrm(24, (DEPTH, D_MODEL), 0.01),
        "w_up": nrm(25, (DEPTH, D_MODEL, 2 * D_FF), D_MODEL ** -0.5),
        "ffn_conv_w": nrm(26, (DEPTH, FFN_CONV, 2 * D_FF), FFN_CONV ** -0.5),
        "ffn_conv_b": nrm(27, (DEPTH, 2 * D_FF), 0.01),
        "w_down": nrm(28, (DEPTH, D_FF, D_MODEL), DN_BETA * D_FF ** -0.5),
        "ln2_g": gain(29, (DEPTH, D_MODEL)),
        "ln2_b": nrm(30, (DEPTH, D_MODEL), 0.01),
    }


def reference(x_prompt, x_sample, cache_k, cache_v, cache_logf, state_conv_ssm, state_ssm, state_conv_ffn,
              page_table, meta_tokens, ln_in_g, ln_in_b, w_in, b_f, w_att_out, ssm_conv_w, ssm_conv_b,
              dt_bias, a_log, d_skip, ssm_norm_w, w_ssm_out, w_o, ln1_g, ln1_b, w_up, ffn_conv_w,
              ffn_conv_b, w_down, ln2_g, ln2_b):
    n_p = x_prompt.shape[0]
    n_s = x_sample.shape[0]
    meta = jnp.broadcast_to(meta_tokens[None].astype(x_prompt.dtype), (n_p, N_META, D_MODEL))
    hp = layer_norm(jnp.concatenate([meta, x_prompt], axis=1), ln_in_g, ln_in_b)
    hs = layer_norm(x_sample, ln_in_g, ln_in_b)
    Lp = hp.shape[1]
    T = hs.shape[1]
    prompt_segments = ((N_META, N_META), (Lp - N_META, SSM_CHUNK))
    sample_segments = ((T, T),)
    p_states = []
    s_states = []
    for l in range(DEPTH):
        lw = (w_in[l], b_f[l], w_att_out[l], ssm_conv_w[l], ssm_conv_b[l], dt_bias[l], a_log[l], d_skip[l],
              ssm_norm_w[l], w_ssm_out[l], w_o[l], ln1_g[l], ln1_b[l], w_up[l], ffn_conv_w[l], ffn_conv_b[l],
              w_down[l], ln2_g[l], ln2_b[l])
        hp, st_p = _layer(hp, fox_prompt,
                          jnp.zeros((n_p, SSM_CONV - 1, XBC_W), hp.dtype),
                          jnp.zeros((n_p, SSM_HEADS, SSM_HEAD_DIM, D_STATE), hp.dtype),
                          jnp.zeros((n_p, FFN_CONV - 1, 2 * D_FF), hp.dtype),
                          prompt_segments, lw)
        attn_s = functools.partial(fox_sample, cache_k=cache_k, cache_v=cache_v, cache_logf=cache_logf,
                                   page_table=page_table, layer=l)
        hs, st_s = _layer(hs, attn_s, state_conv_ssm[l], state_ssm[l], state_conv_ffn[l], sample_segments, lw)
        p_states.append(st_p)
        s_states.append(st_s)

    def stacked(lst, i):
        return jnp.stack([st[i] for st in lst], axis=0)

    k_prompt = stacked(p_states, 0)
    v_prompt = stacked(p_states, 1)
    logf_prompt = stacked(p_states, 2)
    conv_ssm_prompt = stacked(p_states, 3)
    ssm_prompt = stacked(p_states, 4)
    conv_ffn_prompt = stacked(p_states, 5)
    k_sample = stacked(s_states, 0)
    v_sample = stacked(s_states, 1)
    logf_sample = stacked(s_states, 2)
    conv_ssm_sample = stacked(s_states, 3)
    ssm_sample = stacked(s_states, 4)
    conv_ffn_sample = stacked(s_states, 5)
    y_prompt = hp[:, N_META:]
    y_sample = hs
    return (y_prompt, y_sample, k_prompt, v_prompt, logf_prompt, conv_ssm_prompt, ssm_prompt, conv_ffn_prompt,
            k_sample, v_sample, logf_sample, conv_ssm_sample, ssm_sample, conv_ffn_sample)
```

```python
import functools

import jax
import jax.numpy as jnp
from jax import lax
from jax.experimental import pallas as pl
from jax.experimental.pallas import tpu as pltpu

F32 = jnp.float32
BF16 = jnp.bfloat16
LN_EPS = 1e-5
RMS_EPS = 1e-5
NEG = -1e30
LANES = 128
VMEM_LIMIT = 56 * 1024 * 1024


def _cp(*sem):
    return pltpu.CompilerParams(dimension_semantics=sem, vmem_limit_bytes=VMEM_LIMIT)


def _row_tile(m, target, mult=16):
    if m <= target:
        return m
    best = None
    for t in range(mult, target + 1, mult):
        if m % t == 0:
            best = t
    assert best is not None, (m, target)
    return best


def _dot(a, b):
    return jnp.dot(a, b, preferred_element_type=F32)


def _dot_nt(a, b):
    return lax.dot_general(a, b, (((1,), (1,)), ((), ())), preferred_element_type=F32)


def _dot_tn(a, b):
    return lax.dot_general(a, b, (((0,), (0,)), ((), ())), preferred_element_type=F32)


def _split3(x):
    hi = x.astype(BF16)
    r = x - hi.astype(F32)
    mid = r.astype(BF16)
    lo = (r - mid.astype(F32)).astype(BF16)
    return hi, mid, lo


def _sum01_left(m01, x):
    hi, mid, lo = _split3(x)
    return (_dot(m01, lo) + _dot(m01, mid)) + _dot(m01, hi)


def _sum01_right(x, m01):
    hi, mid, lo = _split3(x)
    return (_dot(lo, m01) + _dot(mid, m01)) + _dot(hi, m01)


def _softplus(x):
    return jnp.maximum(x, 0.0) + jnp.log1p(jnp.exp(-jnp.abs(x)))


def _silu(x):
    return x * (1.0 / (1.0 + jnp.exp(-x)))


def _sigmoid(x):
    return 1.0 / (1.0 + jnp.exp(-x))


def _ln_rows(x, g, b):
    mu = jnp.mean(x, -1, keepdims=True)
    xc = x - mu
    var = jnp.mean(xc * xc, -1, keepdims=True)
    return xc * lax.rsqrt(var + LN_EPS) * g + b


def _ln_in_kernel(xp_ref, xs_ref, g_ref, b_ref, hf_ref, hb_ref, *, n_big, m_small):
    i = pl.program_id(0)

    @pl.when(i < n_big)
    def _():
        y = _ln_rows(xp_ref[...], g_ref[...], b_ref[...])
        hf_ref[...] = y
        hb_ref[...] = y.astype(BF16)

    @pl.when(i == n_big)
    def _():
        y = _ln_rows(xs_ref[...], g_ref[...], b_ref[...])
        hf_ref[0:m_small, :] = y
        hb_ref[0:m_small, :] = y.astype(BF16)


def _ln_in(xp, xs, g, b, tr):
    mb, d = xp.shape
    ms = xs.shape[0]
    assert mb % tr == 0 and ms <= tr
    nb = mb // tr
    m = mb + ms
    return pl.pallas_call(
        functools.partial(_ln_in_kernel, n_big=nb, m_small=ms),
        grid=(nb + 1,),
        in_specs=[pl.BlockSpec((tr, d), lambda i: (jnp.minimum(i, nb - 1), 0)),
                  pl.BlockSpec((ms, d), lambda i: (0, 0)),
                  pl.BlockSpec((1, d), lambda i: (0, 0)),
                  pl.BlockSpec((1, d), lambda i: (0, 0))],
        out_specs=[pl.BlockSpec((tr, d), lambda i: (i, 0)),
                   pl.BlockSpec((tr, d), lambda i: (i, 0))],
        out_shape=[jax.ShapeDtypeStruct((m, d), F32), jax.ShapeDtypeStruct((m, d), BF16)],
        compiler_params=_cp("arbitrary"),
        name="ln_in",
    )(xp, xs, g, b)


def _mm_kernel(x_ref, w_ref, o_ref):
    o_ref[...] = _dot(x_ref[...], w_ref[...]).astype(o_ref.dtype)


def _matmul(x, w, tm, tn, out_dtype=F32, name="matmul"):
    m, k = x.shape
    n = w.shape[1]
    assert m % tm == 0 and n % tn == 0
    return pl.pallas_call(
        _mm_kernel,
        grid=(m // tm, n // tn),
        in_specs=[pl.BlockSpec((tm, k), lambda i, j: (i, 0)),
                  pl.BlockSpec((k, tn), lambda i, j: (0, j))],
        out_specs=pl.BlockSpec((tm, tn), lambda i, j: (i, j)),
        out_shape=jax.ShapeDtypeStruct((m, n), out_dtype),
        compiler_params=_cp("parallel", "arbitrary"),
        name=name,
    )(x, w)


def _small_proj_kernel(x_ref, w_ref, b_ref, o_ref, *, n_f):
    a = _dot(x_ref[...], w_ref[...]) + b_ref[...]
    lane = lax.broadcasted_iota(jnp.int32, a.shape, 1)
    sp_pos = _softplus(a)
    ls = -_softplus(-a)
    o_ref[...] = jnp.where(lane < n_f, ls, sp_pos)


def _small_proj(x, w, b, tm, n_f):
    m, k = x.shape
    return pl.pallas_call(
        functools.partial(_small_proj_kernel, n_f=n_f),
        grid=(m // tm,),
        in_specs=[pl.BlockSpec((tm, k), lambda i: (i, 0)),
                  pl.BlockSpec((k, LANES), lambda i: (0, 0)),
                  pl.BlockSpec((1, LANES), lambda i: (0, 0))],
        out_specs=pl.BlockSpec((tm, LANES), lambda i: (i, 0)),
        out_shape=jax.ShapeDtypeStruct((m, LANES), F32),
        compiler_params=_cp("arbitrary"),
        name="small_proj",
    )(x, w, b)


def _cumsum_kernel(x_ref, o_ref, *, nblk):
    h = x_ref.shape[0]
    r = lax.broadcasted_iota(jnp.int32, (LANES, LANES), 0)
    c = lax.broadcasted_iota(jnp.int32, (LANES, LANES), 1)
    tri = jnp.where(r <= c, 1.0, 0.0).astype(BF16)
    carry = jnp.zeros((h, 1), F32)
    for j in range(nblk):
        blk = _sum01_right(x_ref[:, j * LANES:(j + 1) * LANES], tri) + carry
        o_ref[:, j * LANES:(j + 1) * LANES] = blk
        carry = blk[:, LANES - 1:LANES]


def _cumsum_lanes(x):
    n, h, length = x.shape
    return pl.pallas_call(
        functools.partial(_cumsum_kernel, nblk=length // LANES),
        grid=(n,),
        in_specs=[pl.BlockSpec((None, h, length), lambda i: (i, 0, 0))],
        out_specs=pl.BlockSpec((None, h, length), lambda i: (i, 0, 0)),
        out_shape=jax.ShapeDtypeStruct((n, h, length), F32),
        compiler_params=_cp("arbitrary"),
        name="logf_cumsum",
    )(x)


def _fox_prompt_kernel(qr_ref, kr_ref, vr_ref, qm_ref, km_ref, vm_ref, ft_ref, fc_ref,
                       or_ref, om_ref, k_s, v_s, *, nm, bq, scale):
    h = pl.program_id(1)
    seq = qr_ref.shape[0]
    dh = qr_ref.shape[1]
    pad = LANES - nm
    nq = seq // bq

    k_s[0:pad, :] = jnp.zeros((pad, dh), BF16)
    v_s[0:pad, :] = jnp.zeros((pad, dh), BF16)
    k_s[pad:LANES, :] = km_ref[...].astype(BF16)
    v_s[pad:LANES, :] = vm_ref[...].astype(BF16)
    k_s[LANES:, :] = kr_ref[...].astype(BF16)
    v_s[LANES:, :] = vr_ref[...].astype(BF16)

    hsel = lax.broadcasted_iota(jnp.int32, (1, fc_ref.shape[1]), 1) == h

    def fcol(start, size):
        blk = fc_ref[pl.ds(start, size), :]
        return jnp.sum(jnp.where(hsel, blk, 0.0), axis=1, keepdims=True)

    def frow(start, size):
        u0 = start // LANES
        parts = [ft_ref[h, pl.ds(u0 + u, 1), :] for u in range(size // LANES)]
        return parts[0] if len(parts) == 1 else jnp.concatenate(parts, axis=1)

    def step(q, fq, start, size, mask, m, l, acc):
        kb = k_s[pl.ds(start, size), :]
        vb = v_s[pl.ds(start, size), :]
        s = _dot_nt(q, kb) + (fq - frow(start, size))
        if mask is not None:
            s = jnp.where(mask, s, NEG)
        m_new = jnp.maximum(m, jnp.max(s, axis=1, keepdims=True))
        alpha = jnp.exp(m - m_new)
        p = jnp.exp(s - m_new)
        l = alpha * l + jnp.sum(p, axis=1, keepdims=True)
        acc = alpha * acc + _dot(p.astype(BF16), vb)
        return m_new, l, acc

    qm = (qm_ref[...] * scale).astype(BF16)
    rm = lax.broadcasted_iota(jnp.int32, (nm, LANES), 0)
    cm = lax.broadcasted_iota(jnp.int32, (nm, LANES), 1)
    m0 = jnp.full((nm, 1), NEG, F32)
    l0 = jnp.zeros((nm, 1), F32)
    a0 = jnp.zeros((nm, dh), F32)
    _, lm, am = step(qm, fcol(pad, nm), 0, LANES, (cm >= pad) & (cm - pad <= rm), m0, l0, a0)
    om_ref[...] = (am / lm).astype(om_ref.dtype)

    meta_mask = lax.broadcasted_iota(jnp.int32, (bq, LANES), 1) >= pad
    rr = lax.broadcasted_iota(jnp.int32, (bq, bq), 0)
    cc = lax.broadcasted_iota(jnp.int32, (bq, bq), 1)
    diag_mask = cc <= rr

    def qblock(i, _):
        q0 = pl.multiple_of(i * bq, bq)
        q = (qr_ref[pl.ds(q0, bq), :] * scale).astype(BF16)
        fq = fcol(LANES + q0, bq)
        m = jnp.full((bq, 1), NEG, F32)
        l = jnp.zeros((bq, 1), F32)
        acc = jnp.zeros((bq, dh), F32)
        m, l, acc = step(q, fq, 0, LANES, meta_mask, m, l, acc)

        def kblock(j, c):
            return step(q, fq, pl.multiple_of(LANES + j * bq, LANES), bq, None, *c)

        m, l, acc = lax.fori_loop(0, i, kblock, (m, l, acc))
        m, l, acc = step(q, fq, LANES + q0, bq, diag_mask, m, l, acc)
        or_ref[pl.ds(q0, bq), :] = (acc / l).astype(or_ref.dtype)
        return 0

    lax.fori_loop(0, nq, qblock, 0)


def _fox_prompt(p_main, f_t, f_c, *, n_b, seq, nm, n_h, dh, q_off, k_off, v_off, aw):
    bq = 256 if seq % 256 == 0 else LANES
    qb, kb, vb = q_off // dh, k_off // dh, v_off // dh
    mrow = (n_b * seq) // nm
    lpx = LANES + seq
    real = lambda cb: pl.BlockSpec((seq, dh), lambda n, h: (n, cb + h))
    meta = lambda cb: pl.BlockSpec((nm, dh), lambda n, h: (mrow + n, cb + h))
    return pl.pallas_call(
        functools.partial(_fox_prompt_kernel, nm=nm, bq=bq, scale=dh ** -0.5),
        grid=(n_b, n_h),
        in_specs=[real(qb), real(kb), real(vb), meta(qb), meta(kb), meta(vb),
                  pl.BlockSpec((None, n_h, lpx // LANES, LANES), lambda n, h: (n, 0, 0, 0)),
                  pl.BlockSpec((None, lpx, n_h), lambda n, h: (n, 0, 0))],
        out_specs=[pl.BlockSpec((seq, dh), lambda n, h: (n, h)),
                   pl.BlockSpec((nm, dh), lambda n, h: (n, h))],
        out_shape=[jax.ShapeDtypeStruct((n_b * seq, aw), BF16),
                   jax.ShapeDtypeStruct((n_b * nm, aw), BF16)],
        scratch_shapes=[pltpu.VMEM((lpx, dh), BF16), pltpu.VMEM((lpx, dh), BF16)],
        compiler_params=_cp("parallel", "arbitrary"),
        name="fox_prompt",
    )(p_main, p_main, p_main, p_main, p_main, p_main, f_t, f_c)


def _fox_sample_kernel(pt_ref, q_ref, kn_ref, vn_ref, lfn_ref, *rest, n_h, pps):
    k_refs = rest[0:pps]
    v_refs = rest[pps:2 * pps]
    lf_refs = rest[2 * pps:3 * pps]
    o_ref = rest[3 * pps]
    q_s, mb_s, m_s, l_s, acc_s, car_s, kn_s, vn_s = rest[3 * pps + 1:]
    g = pl.program_id(1)
    ng = pl.num_programs(1)
    nq, dh = q_ref.shape
    nph = lf_refs[0].shape[0]

    ci = lax.broadcasted_iota(jnp.int32, (LANES, LANES), 0)
    cj = lax.broadcasted_iota(jnp.int32, (LANES, LANES), 1)
    same_h = (ci % n_h) == (cj % n_h)
    m12 = jnp.concatenate([jnp.where(same_h & (ci // n_h > cj // n_h), 1.0, 0.0),
                           jnp.where(same_h, 1.0, 0.0)], axis=1).astype(BF16)

    def suffix_rows(lf):
        r = lf.shape[0]
        hi, mid, lo = _split3(lf)
        res = _dot(jnp.concatenate([lo, mid, hi], axis=0), m12)
        res = (res[0:r] + res[r:2 * r]) + res[2 * r:3 * r]
        return res[:, 0:LANES], res[:, LANES:2 * LANES]

    def update(s, vb):
        m_old = m_s[...]
        m_new = jnp.maximum(m_old, jnp.max(s, axis=1, keepdims=True))
        alpha = jnp.exp(m_old - m_new)
        p = jnp.exp(s - m_new)
        l_s[...] = alpha * l_s[...] + jnp.sum(p, axis=1, keepdims=True)
        acc_s[...] = alpha * acc_s[...] + _dot(p.astype(BF16), vb)
        m_s[...] = m_new

    @pl.when(g == 0)
    def _():
        q_s[...] = (q_ref[...] * (dh ** -0.5)).astype(BF16)
        kn_s[...] = jnp.zeros(kn_s.shape, BF16)
        vn_s[...] = jnp.zeros(vn_s.shape, BF16)
        kn_s[0:nq, :] = kn_ref[...].astype(BF16)
        vn_s[0:nq, :] = vn_ref[...].astype(BF16)
        within, total = suffix_rows(lfn_ref[...])
        sfx = within[0:1, :]
        ri = lax.broadcasted_iota(jnp.int32, (nq, LANES), 0)
        li = lax.broadcasted_iota(jnp.int32, (nq, LANES), 1)
        rowc = -jnp.sum(jnp.where(ri == li, sfx, 0.0), axis=1, keepdims=True)
        head_ok = (ri % n_h) == (li % n_h)
        mb_s[...] = jnp.where(head_ok, rowc, NEG)
        m_s[...] = jnp.full(m_s.shape, NEG, F32)
        l_s[...] = jnp.zeros(l_s.shape, F32)
        acc_s[...] = jnp.zeros(acc_s.shape, F32)
        s = _dot_nt(q_s[...], kn_s[...]) + sfx + mb_s[...]
        s = jnp.where(li // n_h <= ri // n_h, s, NEG)
        update(s, vn_s[...])
        car_s[...] = total[0:1, :]

    q = q_s[...]
    mb = mb_s[...]
    rows = lax.broadcasted_iota(jnp.int32, (nph, LANES), 0)
    for j in range(pps):
        within, total = suffix_rows(lf_refs[j][...])
        later = jnp.zeros((nph, LANES), F32)
        for r in range(1, nph):
            later = later + jnp.where(rows < r, total[r:r + 1, :], 0.0)
        bias = within + later + car_s[...]
        car_s[...] = car_s[...] + jnp.sum(total, axis=0, keepdims=True)
        kb = k_refs[j][...].astype(BF16)
        s = _dot_nt(q, kb)
        s = jnp.concatenate(
            [s[:, r * LANES:(r + 1) * LANES] + (bias[r:r + 1, :] + mb) for r in range(nph)], axis=1)
        update(s, v_refs[j][...].astype(BF16))

    @pl.when(g == ng - 1)
    def _():
        o_ref[...] = acc_s[...] / l_s[...]


def _fox_sample(q, kn, vn, lfn, ck, cv, clf, page_table, *, n_h, pps):
    nb, nq, dh = q.shape
    npg = page_table.shape[1]
    assert npg % pps == 0 and nq <= LANES and LANES % n_h == 0
    prow = ck.shape[1]
    nph = clf.shape[1]
    seq_spec = lambda r: pl.BlockSpec((None, r, dh), lambda b, g, pt: (b, 0, 0))

    def page_spec(rows, width, j):
        return pl.BlockSpec((None, rows, width), lambda b, g, pt: (pt[b, npg - 1 - (g * pps + j)], 0, 0))

    in_specs = [seq_spec(nq), seq_spec(nq), seq_spec(nq), pl.BlockSpec((None, 16, LANES), lambda b, g, pt: (b, 0, 0))]
    in_specs += [page_spec(prow, dh, j) for j in range(pps)]
    in_specs += [page_spec(prow, dh, j) for j in range(pps)]
    in_specs += [page_spec(nph, LANES, j) for j in range(pps)]
    grid_spec = pltpu.PrefetchScalarGridSpec(
        num_scalar_prefetch=1,
        grid=(nb, npg // pps),
        in_specs=in_specs,
        out_specs=pl.BlockSpec((None, nq, dh), lambda b, g, pt: (b, 0, 0)),
        scratch_shapes=[pltpu.VMEM((nq, dh), BF16), pltpu.VMEM((nq, LANES), F32),
                        pltpu.VMEM((nq, 1), F32), pltpu.VMEM((nq, 1), F32), pltpu.VMEM((nq, dh), F32),
                        pltpu.VMEM((1, LANES), F32),
                        pltpu.VMEM((LANES, dh), BF16), pltpu.VMEM((LANES, dh), BF16)],
    )
    return pl.pallas_call(
        functools.partial(_fox_sample_kernel, n_h=n_h, pps=pps),
        grid_spec=grid_spec,
        out_shape=jax.ShapeDtypeStruct((nb, nq, dh), F32),
        compiler_params=_cp("parallel", "arbitrary"),
        name="fox_sample",
    )(page_table, q, kn, vn, lfn, *([ck] * pps), *([cv] * pps), *([clf] * pps))


CH = 128


def _ssd_kernel(xs_ref, b_ref, c_ref, dtc_ref, dtr_ref, alc_ref, alr_ref, dsk_ref,
                wx_ref, wb_ref, wc_ref, bx_ref, bb_ref, bc_ref, px_ref, pb_ref, pc_ref, h0_ref,
                y_ref, hT_ref, ex_s, eb_s, ec_s, h_s, *, t_valid, n_e, p_dim):
    c = pl.program_id(2)
    nc = pl.num_programs(2)
    kw = wx_ref.shape[0]
    base = 8

    @pl.when(c == 0)
    def _():
        for e_s, p_ref in ((ex_s, px_ref), (eb_s, pb_ref), (ec_s, pc_ref)):
            e_s[...] = jnp.zeros(e_s.shape, F32)
            e_s[base - (kw - 1):base, :] = p_ref[...]
        h_s[...] = h0_ref[...].reshape(h_s.shape)

    def conv_silu(e_s, u_ref, w_ref, bias_ref):
        e_s[base:base + t_valid, :] = u_ref[...]
        out = bias_ref[...]
        for j in range(kw):
            out = out + e_s[base - (kw - 1) + j:base - (kw - 1) + j + CH, :] * w_ref[j:j + 1, :]
        return _silu(out)

    xc = conv_silu(ex_s, xs_ref, wx_ref, bx_ref)
    bm = conv_silu(eb_s, b_ref, wb_ref, bb_ref)
    cm = conv_silu(ec_s, c_ref, wc_ref, bc_ref)
    if t_valid == CH:
        for e_s in (ex_s, eb_s, ec_s):
            e_s[0:base, :] = e_s[CH:CH + base, :]

    row_c = lax.broadcasted_iota(jnp.int32, (CH, 1), 0)
    lane_r = lax.broadcasted_iota(jnp.int32, (1, CH), 1)
    dtc = dtc_ref[...]
    dtr = dtr_ref[...]
    if t_valid < CH:
        xc = jnp.where(row_c < t_valid, xc, 0.0)
    a_c = dtc * (-jnp.exp(alc_ref[...]))
    a_r = dtr * (-jnp.exp(alr_ref[...]))
    ri = lax.broadcasted_iota(jnp.int32, (CH, CH), 0)
    li = lax.broadcasted_iota(jnp.int32, (CH, CH), 1)
    causal = li <= ri
    tril = jnp.where(causal, 1.0, 0.0).astype(BF16)
    triu = jnp.where(ri <= li, 1.0, 0.0).astype(BF16)
    cum_c = _sum01_left(tril, a_c)
    cum_r = _sum01_right(a_r, triu)
    end_c = cum_c[CH - 1:CH, :]
    dd_c = jnp.exp(end_c - cum_c) * dtc
    ecum_c = jnp.exp(cum_c)
    cdec_r = jnp.exp(cum_r[:, CH - 1:CH])

    xb = xc.astype(BF16)
    bmb = bm.astype(BF16)
    cmb = cm.astype(BF16)
    cb = _dot_nt(cmb, bmb)
    lane2 = lax.broadcasted_iota(jnp.int32, (CH, 2 * p_dim), 1) < p_dim
    row2 = lax.broadcasted_iota(jnp.int32, (2 * p_dim, 1), 0) < p_dim
    ys = []
    for pr in range(n_e // 2):
        e0, e1 = 2 * pr, 2 * pr + 1
        xp = xb[:, e0 * p_dim:(e1 + 1) * p_dim]
        xpf = xc[:, e0 * p_dim:(e1 + 1) * p_dim]
        yd = []
        for e in (e0, e1):
            seg = cum_c[:, e:e + 1] - cum_r[e:e + 1, :]
            dec = jnp.exp(jnp.where(causal, seg, -jnp.inf))
            w = (cb * dec * dtr[e:e + 1, :]).astype(BF16)
            yd.append(_dot(w, xp))
        y_diag = jnp.where(lane2, yd[0], yd[1])
        hp = h_s[e0 * p_dim:(e1 + 1) * p_dim, :]
        y_off = _dot_nt(cmb, hp.astype(BF16)) * jnp.where(lane2, ecum_c[:, e0:e0 + 1], ecum_c[:, e1:e1 + 1])
        ys.append(y_diag + y_off + xpf * dsk_ref[:, e0 * p_dim:(e1 + 1) * p_dim])
        xw = (xpf * jnp.where(lane2, dd_c[:, e0:e0 + 1], dd_c[:, e1:e1 + 1])).astype(BF16)
        st = _dot_tn(xw, bmb)
        cd = jnp.where(row2, cdec_r[e0:e0 + 1, :], cdec_r[e1:e1 + 1, :])
        h_s[e0 * p_dim:(e1 + 1) * p_dim, :] = hp * cd + st
    y = jnp.concatenate(ys, axis=1)
    y_ref[...] = y[0:t_valid, :]

    @pl.when(c == nc - 1)
    def _():
        hT_ref[...] = h_s[...].reshape(hT_ref.shape)


def _ssd(xsrc, col_x, col_b, col_c, t_valid, n_seq, n_chunks, dtc, dtr, alc, alr, dsk,
         conv_w, conv_b, prev, h0, *, n_g, n_e, p_dim, n_state, d_inner):
    gw = n_e * p_dim
    kw = conv_w.shape[0]
    three_d = xsrc.ndim == 3
    assert three_d or t_valid == CH
    row_of = lambda n, c: n * n_chunks + c

    def src_spec(width, colblk):
        if three_d:
            return pl.BlockSpec((None, t_valid, width), lambda n, g, c: (n, 0, colblk(g)))
        return pl.BlockSpec((CH, width), lambda n, g, c: (row_of(n, c), colblk(g)))

    cx = lambda g: col_x // gw + g
    cbk = lambda g: col_b // n_state + g
    cck = lambda g: col_c // n_state + g
    wx = lambda g: g
    wb = lambda g: d_inner // n_state + g
    wc = lambda g: (d_inner + n_g * n_state) // n_state + g

    if three_d:
        dtc_spec = pl.BlockSpec((None, None, CH, n_e), lambda n, g, c: (g, n, 0, 0))
        dtr_spec = pl.BlockSpec((None, n_e, CH), lambda n, g, c: (n, g, 0))
        y_spec = pl.BlockSpec((None, t_valid, gw), lambda n, g, c: (n, 0, g))
        y_shape = jax.ShapeDtypeStruct((n_seq, t_valid, d_inner), F32)
    else:
        dtc_spec = pl.BlockSpec((None, CH, n_e), lambda n, g, c: (g, row_of(n, c), 0))
        dtr_spec = pl.BlockSpec((n_e, CH), lambda n, g, c: (g, row_of(n, c)))
        y_spec = pl.BlockSpec((CH, gw), lambda n, g, c: (row_of(n, c), g))
        y_shape = jax.ShapeDtypeStruct((n_seq * n_chunks * CH, d_inner), F32)

    in_specs = [
        src_spec(gw, cx), src_spec(n_state, cbk), src_spec(n_state, cck),
        dtc_spec, dtr_spec,
        pl.BlockSpec((None, 1, n_e), lambda n, g, c: (g, 0, 0)),
        pl.BlockSpec((n_e, 1), lambda n, g, c: (g, 0)),
        pl.BlockSpec((1, gw), lambda n, g, c: (0, g)),
        pl.BlockSpec((kw, gw), lambda n, g, c: (0, wx(g))),
        pl.BlockSpec((kw, n_state), lambda n, g, c: (0, wb(g))),
        pl.BlockSpec((kw, n_state), lambda n, g, c: (0, wc(g))),
        pl.BlockSpec((1, gw), lambda n, g, c: (0, wx(g))),
        pl.BlockSpec((1, n_state), lambda n, g, c: (0, wb(g))),
        pl.BlockSpec((1, n_state), lambda n, g, c: (0, wc(g))),
        pl.BlockSpec((None, kw - 1, gw), lambda n, g, c: (n, 0, wx(g))),
        pl.BlockSpec((None, kw - 1, n_state), lambda n, g, c: (n, 0, wb(g))),
        pl.BlockSpec((None, kw - 1, n_state), lambda n, g, c: (n, 0, wc(g))),
        pl.BlockSpec((None, n_e, p_dim, n_state), lambda n, g, c: (n, g, 0, 0)),
    ]
    out_specs = [y_spec, pl.BlockSpec((None, n_e, p_dim, n_state), lambda n, g, c: (n, g, 0, 0))]
    out_shape = [y_shape, jax.ShapeDtypeStruct((n_seq, n_g * n_e, p_dim, n_state), F32)]
    return pl.pallas_call(
        functools.partial(_ssd_kernel, t_valid=t_valid, n_e=n_e, p_dim=p_dim),
        grid=(n_seq, n_g, n_chunks),
        in_specs=in_specs, out_specs=out_specs, out_shape=out_shape,
        scratch_shapes=[pltpu.VMEM((CH + 8, gw), F32), pltpu.VMEM((CH + 8, n_state), F32),
                        pltpu.VMEM((CH + 8, n_state), F32), pltpu.VMEM((n_e * p_dim, n_state), F32)],
        compiler_params=_cp("parallel", "parallel", "arbitrary"),
        name="ssd",
    )(xsrc, xsrc, xsrc, dtc, dtr, alc, alr, dsk, conv_w, conv_w, conv_w, conv_b, conv_b, conv_b,
      prev, prev, prev, h0)


def _gated_norm_kernel(y_ref, z_ref, w_ref, o_ref):
    g = y_ref[...] * _silu(z_ref[...])
    ms = jnp.mean(g * g, -1, keepdims=True)
    o_ref[...] = (g * lax.rsqrt(ms + RMS_EPS) * w_ref[...]).astype(o_ref.dtype)


def _gated_norm(y, p_main, z_off, w, tm):
    m, di = y.shape
    assert z_off % di == 0
    zb = z_off // di
    return pl.pallas_call(
        _gated_norm_kernel,
        grid=(m // tm,),
        in_specs=[pl.BlockSpec((tm, di), lambda i: (i, 0)),
                  pl.BlockSpec((tm, di), lambda i: (i, zb)),
                  pl.BlockSpec((1, di), lambda i: (0, 0))],
        out_specs=pl.BlockSpec((tm, di), lambda i: (i, 0)),
        out_shape=jax.ShapeDtypeStruct((m, di), BF16),
        compiler_params=_cp("arbitrary"),
        name="gated_rmsnorm",
    )(y, p_main, w)


def _proj_gate_kernel(x_ref, w_ref, g_ref, *rest, add):
    if add:
        a_ref, o_ref = rest
    else:
        (o_ref,) = rest
    v = _sigmoid(g_ref[...]) * _dot(x_ref[...], w_ref[...])
    if add:
        v = v + a_ref[...]
    o_ref[...] = v.astype(o_ref.dtype)


def _proj_gate(x, w, p_main, g_off, addend, tm, tn, out_dtype):
    m, k = x.shape
    n = w.shape[1]
    assert g_off % tn == 0 and n % tn == 0
    gb = g_off // tn
    in_specs = [pl.BlockSpec((tm, k), lambda i, j: (i, 0)),
                pl.BlockSpec((k, tn), lambda i, j: (0, j)),
                pl.BlockSpec((tm, tn), lambda i, j: (i, gb + j))]
    args = [x, w, p_main]
    if addend is not None:
        in_specs.append(pl.BlockSpec((tm, tn), lambda i, j: (i, j)))
        args.append(addend)
    return pl.pallas_call(
        functools.partial(_proj_gate_kernel, add=addend is not None),
        grid=(m // tm, n // tn),
        in_specs=in_specs,
        out_specs=pl.BlockSpec((tm, tn), lambda i, j: (i, j)),
        out_shape=jax.ShapeDtypeStruct((m, n), out_dtype),
        compiler_params=_cp("parallel", "arbitrary"),
        name="proj_gate",
    )(*args)


def _mm_res_ln_kernel(x_ref, w_ref, r_ref, g_ref, b_ref, of_ref, ob_ref, acc_s, *, alpha):
    k = pl.program_id(1)

    @pl.when(k == 0)
    def _():
        acc_s[...] = alpha * r_ref[...]

    acc_s[...] += _dot(x_ref[...], w_ref[...])

    @pl.when(k == pl.num_programs(1) - 1)
    def _():
        y = _ln_rows(acc_s[...], g_ref[...], b_ref[...])
        of_ref[...] = y
        ob_ref[...] = y.astype(BF16)


def _mm_res_ln(x, w, res, g, b, alpha, tm, tk):
    m, k = x.shape
    n = w.shape[1]
    assert m % tm == 0 and k % tk == 0
    return pl.pallas_call(
        functools.partial(_mm_res_ln_kernel, alpha=alpha),
        grid=(m // tm, k // tk),
        in_specs=[pl.BlockSpec((tm, tk), lambda i, kk: (i, kk)),
                  pl.BlockSpec((tk, n), lambda i, kk: (kk, 0)),
                  pl.BlockSpec((tm, n), lambda i, kk: (i, 0)),
                  pl.BlockSpec((1, n), lambda i, kk: (0, 0)),
                  pl.BlockSpec((1, n), lambda i, kk: (0, 0))],
        out_specs=[pl.BlockSpec((tm, n), lambda i, kk: (i, 0)), pl.BlockSpec((tm, n), lambda i, kk: (i, 0))],
        out_shape=[jax.ShapeDtypeStruct((m, n), F32), jax.ShapeDtypeStruct((m, n), BF16)],
        scratch_shapes=[pltpu.VMEM((tm, n), F32)],
        compiler_params=_cp("parallel", "arbitrary"),
        name="mm_res_ln",
    )(x, w, res, g, b)


def _ffn_conv_big_kernel(ua_ref, ub_ref, wa_ref, wb_ref, ba_ref, bb_ref, pa_ref, pb_ref, o_ref, ea_s, eb_s):
    c = pl.program_id(2)
    kw = wa_ref.shape[0]
    rows = ua_ref.shape[0]
    base = 8

    @pl.when(c == 0)
    def _():
        ea_s[base - (kw - 1):base, :] = pa_ref[...]
        eb_s[base - (kw - 1):base, :] = pb_ref[...]

    def conv(e_s, u_ref, w_ref, bias_ref):
        e_s[base:base + rows, :] = u_ref[...]
        out = bias_ref[...]
        for j in range(kw):
            out = out + e_s[base - (kw - 1) + j:base - (kw - 1) + j + rows, :] * w_ref[j:j + 1, :]
        e_s[0:base, :] = e_s[rows:rows + base, :]
        return out

    a = conv(ea_s, ua_ref, wa_ref, ba_ref)
    b = conv(eb_s, ub_ref, wb_ref, bb_ref)
    o_ref[...] = (_silu(a) * b).astype(o_ref.dtype)


def _ffn_conv_big(u, conv_w, conv_b, prev, *, n_seq, seq, dff, rows, tf):
    kw = conv_w.shape[0]
    nj = dff // tf
    nc = seq // rows
    return pl.pallas_call(
        _ffn_conv_big_kernel,
        grid=(n_seq, nj, nc),
        in_specs=[pl.BlockSpec((rows, tf), lambda n, j, c: (n * nc + c, j)),
                  pl.BlockSpec((rows, tf), lambda n, j, c: (n * nc + c, nj + j)),
                  pl.BlockSpec((kw, tf), lambda n, j, c: (0, j)),
                  pl.BlockSpec((kw, tf), lambda n, j, c: (0, nj + j)),
                  pl.BlockSpec((1, tf), lambda n, j, c: (0, j)),
                  pl.BlockSpec((1, tf), lambda n, j, c: (0, nj + j)),
                  pl.BlockSpec((None, kw - 1, tf), lambda n, j, c: (n, 0, j)),
                  pl.BlockSpec((None, kw - 1, tf), lambda n, j, c: (n, 0, nj + j))],
        out_specs=pl.BlockSpec((rows, tf), lambda n, j, c: (n * nc + c, j)),
        out_shape=jax.ShapeDtypeStruct((n_seq * seq, dff), BF16),
        scratch_shapes=[pltpu.VMEM((rows + 8, tf), F32), pltpu.VMEM((rows + 8, tf), F32)],
        compiler_params=_cp("parallel", "parallel", "arbitrary"),
        name="ffn_conv_big",
    )(u, u, conv_w, conv_w, conv_b, conv_b, prev, prev)


def _ffn_conv_small_kernel(*refs, kw):
    ua = refs[0:kw]
    ub = refs[kw:2 * kw]
    wa_ref, wb_ref, ba_ref, bb_ref, o_ref = refs[2 * kw:]
    a = ba_ref[...]
    b = bb_ref[...]
    for j in range(kw):
        a = a + ua[j][...] * wa_ref[j:j + 1, :]
        b = b + ub[j][...] * wb_ref[j:j + 1, :]
    o_ref[...] = (_silu(a) * b).astype(o_ref.dtype)


def _ffn_conv_small(taps, conv_w, conv_b, *, dff, tf):
    kw = conv_w.shape[0]
    rows = taps[0].shape[0]
    nj = dff // tf
    a_specs = [pl.BlockSpec((rows, tf), lambda j: (0, j)) for _ in range(kw)]
    b_specs = [pl.BlockSpec((rows, tf), lambda j: (0, nj + j)) for _ in range(kw)]
    return pl.pallas_call(
        functools.partial(_ffn_conv_small_kernel, kw=kw),
        grid=(nj,),
        in_specs=a_specs + b_specs + [pl.BlockSpec((kw, tf), lambda j: (0, j)),
                                      pl.BlockSpec((kw, tf), lambda j: (0, nj + j)),
                                      pl.BlockSpec((1, tf), lambda j: (0, j)),
                                      pl.BlockSpec((1, tf), lambda j: (0, nj + j))],
        out_specs=pl.BlockSpec((rows, tf), lambda j: (0, j)),
        out_shape=jax.ShapeDtypeStruct((rows, dff), BF16),
        compiler_params=_cp("arbitrary"),
        name="ffn_conv_small",
    )(*taps, *taps, conv_w, conv_w, conv_b, conv_b)


def _shift_taps(u_seq, prev):
    n, t, c = u_seq.shape
    k1 = prev.shape[1]
    full = jnp.concatenate([prev.astype(u_seq.dtype), u_seq], axis=1)
    return [full[:, j:j + t].reshape(n * t, c) for j in range(k1 + 1)]


def _layer(hf, hb, lw, st, dims, page_table):
    (w_in, b_f, w_att_out, ssm_conv_w, ssm_conv_b, dt_bias, a_log, d_skip, ssm_norm_w,
     w_ssm_out, w_o, ln1_g, ln1_b, w_up, ffn_conv_w, ffn_conv_b, w_down, ln2_g, ln2_b) = lw
    cache_k, cache_v, cache_logf, conv_ssm_s, ssm_s, conv_ffn_s = st
    d = dims
    B, SEQ, NM, NB, T, D = d["B"], d["SEQ"], d["NM"], d["NB"], d["T"], d["D"]
    H, DH, AW, DI, XBC, SH = d["H"], d["DH"], d["AW"], d["DI"], d["XBC"], d["SH"]
    P, N, G, E, DFF, alpha = d["P"], d["N"], d["G"], d["E"], d["DFF"], d["alpha"]
    M = hf.shape[0]
    MB, MM, MS = B * SEQ, B * NM, NB * T
    tm = _row_tile(M, 1392)
    tm_s = _row_tile(M, 512)

    o = [0]
    for s in (AW, AW, AW, H, DI, XBC, SH, D):
        o.append(o[-1] + s)
    wq, wk, wv, wf, wz, wx, wdt, wga, wgs = (w_in[:, o[i]:o[i] + s] for i, s in
                                              enumerate((AW, AW, AW, H, DI, XBC, SH, D, D)))
    w_main = jnp.concatenate([wz, wq, wk, wv, wx, wga, wgs], axis=1).astype(BF16)
    z_off, q_off, k_off, v_off = 0, DI, DI + AW, DI + 2 * AW
    x_off = DI + 3 * AW
    ga_off = x_off + XBC
    gs_off = ga_off + D
    nmain = gs_off + D
    w_small = jnp.concatenate([wf, wdt, jnp.zeros((D, LANES - H - SH), F32)], axis=1).astype(BF16)
    b_small = jnp.concatenate([b_f, dt_bias, jnp.zeros((LANES - H - SH,), F32)])[None, :]

    p_main = _matmul(hb, w_main, tm, 512, F32, name="in_proj")
    s_small = _small_proj(hb, w_small, b_small, tm, H)
    logf = s_small[:, :H]
    dt = s_small[:, H:H + SH]

    pad = LANES - NM
    lf_real = logf[:MB].reshape(B, SEQ, H)
    lf_meta = logf[MB:MB + MM].reshape(B, NM, H)
    lf_ext = jnp.concatenate([jnp.zeros((B, pad, H), F32), lf_meta, lf_real], axis=1)
    f_t = _cumsum_lanes(lf_ext.transpose(0, 2, 1))
    f_c = f_t.transpose(0, 2, 1)
    f_t = f_t.reshape(B, H, (LANES + SEQ) // LANES, LANES)
    o_real, o_meta = _fox_prompt(p_main, f_t, f_c, n_b=B, seq=SEQ, nm=NM, n_h=H, dh=DH,
                                 q_off=q_off, k_off=k_off, v_off=v_off, aw=AW)

    p_s = p_main[MB + MM:]
    q_s = p_s[:, q_off:q_off + AW].reshape(NB, T * H, DH)
    k_s = p_s[:, k_off:k_off + AW].reshape(NB, T * H, DH)
    v_s = p_s[:, v_off:v_off + AW].reshape(NB, T * H, DH)
    lf_s = logf[MB + MM:].reshape(NB, T * H)
    lfn = jnp.zeros((NB, 16, LANES), F32).at[:, 0, :T * H].set(lf_s)
    pool, PAGE = cache_k.shape[0], cache_k.shape[1]
    ck = cache_k.reshape(pool, PAGE * H, DH)
    cv = cache_v.reshape(pool, PAGE * H, DH)
    clf = cache_logf.astype(F32).reshape(pool, (PAGE * H) // LANES, LANES)
    npg = page_table.shape[1]
    pps = 4 if npg % 4 == 0 else (2 if npg % 2 == 0 else 1)
    o_s = _fox_sample(q_s, k_s, v_s, lfn, ck, cv, clf, page_table, n_h=H, pps=pps)
    o_samp = o_s.reshape(NB * T, AW).astype(BF16)
    att = jnp.concatenate([o_real, o_meta, o_samp], axis=0)

    dt_c = dt.reshape(M, G, E).transpose(1, 0, 2)
    dt_r = dt.T
    alc = a_log.reshape(G, 1, E)
    alr = a_log.reshape(SH, 1)
    dsk = jnp.repeat(d_skip, P)[None, :]
    cbias = ssm_conv_b[None, :]
    kw = ssm_conv_w.shape[0]
    ssd = functools.partial(_ssd, n_g=G, n_e=E, p_dim=P, n_state=N, d_inner=DI)
    col_b = x_off + DI
    col_c = col_b + G * N
    zeros_prev = jnp.zeros((B, kw - 1, XBC), F32)
    zeros_h = jnp.zeros((B, SH, P, N), F32)

    def short_dt(lo, n_seq, t):
        c3 = dt_c[:, lo:lo + n_seq * t].reshape(G, n_seq, t, E)
        r3 = dt_r[:, lo:lo + n_seq * t].reshape(SH, n_seq, t).transpose(1, 0, 2)
        return (jnp.pad(c3, ((0, 0), (0, 0), (0, CH - t), (0, 0))), jnp.pad(r3, ((0, 0), (0, 0), (0, CH - t))))

    p_m3 = p_main[MB:MB + MM].reshape(B, NM, nmain)
    dt_c_m, dt_r_m = short_dt(MB, B, NM)
    y_meta, h_meta = ssd(p_m3, x_off, col_b, col_c, NM, B, 1, dt_c_m, dt_r_m, alc, alr, dsk,
                         ssm_conv_w, cbias, zeros_prev, zeros_h)
    prev_real = p_m3[:, NM - (kw - 1):, x_off:x_off + XBC]
    y_real, h_real = ssd(p_main, x_off, col_b, col_c, CH, B, SEQ // CH, dt_c, dt_r, alc, alr, dsk,
                         ssm_conv_w, cbias, prev_real, h_meta)
    p_s3 = p_s.reshape(NB, T, nmain)
    dt_c_s, dt_r_s = short_dt(MB + MM, NB, T)
    y_samp, h_samp = ssd(p_s3, x_off, col_b, col_c, T, NB, 1, dt_c_s, dt_r_s, alc, alr, dsk,
                         ssm_conv_w, cbias, conv_ssm_s, ssm_s)
    y_all = jnp.concatenate([y_real, y_meta.reshape(MM, DI), y_samp.reshape(MS, DI)], axis=0)
    xbc_real = p_main[:MB, x_off:x_off + XBC].reshape(B, SEQ, XBC)
    conv_ssm_p = xbc_real[:, SEQ - (kw - 1):]
    xbc_s = p_s3[:, :, x_off:x_off + XBC]
    conv_ssm_new_s = jnp.concatenate([conv_ssm_s, xbc_s], axis=1)[:, T:]

    yn = _gated_norm(y_all, p_main, z_off, ssm_norm_w[None, :], tm_s)

    tn_o = 512 if D % 512 == 0 else D
    tm_p = _row_tile(M, 928)
    m1 = _proj_gate(att, w_att_out.astype(BF16), p_main, ga_off, None, tm_p, tn_o, F32)
    merged = _proj_gate(yn, w_ssm_out.astype(BF16), p_main, gs_off, m1, tm_p, tn_o, BF16)
    x1f, x1b = _mm_res_ln(merged, w_o.astype(BF16), hf, ln1_g[None, :], ln1_b[None, :], alpha, tm_s, 512)

    u = _matmul(x1b, w_up.astype(BF16), tm, 512, F32, name="ffn_up")
    kf = ffn_conv_w.shape[0]
    fbias = ffn_conv_b[None, :]
    tf = 512 if DFF % 512 == 0 else DFF
    u_meta = u[MB:MB + MM].reshape(B, NM, 2 * DFF)
    u_samp = u[MB + MM:].reshape(NB, T, 2 * DFF)
    rows_big = 512 if SEQ % 512 == 0 else SEQ
    g_big = _ffn_conv_big(u, ffn_conv_w, fbias, u_meta[:, NM - (kf - 1):], n_seq=B, seq=SEQ, dff=DFF,
                          rows=rows_big, tf=tf)
    taps_m = _shift_taps(u_meta, jnp.zeros((B, kf - 1, 2 * DFF), F32))
    taps_s = _shift_taps(u_samp, conv_ffn_s)
    taps = [jnp.concatenate([a, b], axis=0) for a, b in zip(taps_m, taps_s)]
    g_small = _ffn_conv_small(taps, ffn_conv_w, fbias, dff=DFF, tf=tf)
    gact = jnp.concatenate([g_big, g_small], axis=0)
    x2f, x2b = _mm_res_ln(gact, w_down.astype(BF16), x1f, ln2_g[None, :], ln2_b[None, :], alpha, tm_s, 512)

    conv_ffn_p = jnp.stack([u[(n + 1) * SEQ - (kf - 1):(n + 1) * SEQ] for n in range(B)], axis=0)
    conv_ffn_new_s = jnp.concatenate([conv_ffn_s, u_samp], axis=1)[:, T:]

    def prompt_rows(col_off, width):
        real = p_main[:MB, col_off:col_off + width].reshape(B, SEQ, width)
        meta = p_main[MB:MB + MM, col_off:col_off + width].reshape(B, NM, width)
        return jnp.concatenate([meta, real], axis=1)

    k_p = prompt_rows(k_off, AW).reshape(B, NM + SEQ, H, DH)
    v_p = prompt_rows(v_off, AW).reshape(B, NM + SEQ, H, DH)
    lf_p = jnp.concatenate([lf_meta, lf_real], axis=1)
    k_sm = p_s[:, k_off:k_off + AW].reshape(NB, T, H, DH)
    v_sm = p_s[:, v_off:v_off + AW].reshape(NB, T, H, DH)
    lf_sm = logf[MB + MM:].reshape(NB, T, H)
    states_p = (k_p, v_p, lf_p, conv_ssm_p, h_real, conv_ffn_p)
    states_s = (k_sm, v_sm, lf_sm, conv_ssm_new_s, h_samp, conv_ffn_new_s)
    return x2f, x2b, states_p, states_s


def kernel(x_prompt, x_sample, cache_k, cache_v, cache_logf, state_conv_ssm, state_ssm, state_conv_ffn, page_table, meta_tokens, ln_in_g, ln_in_b, w_in, b_f, w_att_out, ssm_conv_w, ssm_conv_b, dt_bias, a_log, d_skip, ssm_norm_w, w_ssm_out, w_o, ln1_g, ln1_b, w_up, ffn_conv_w, ffn_conv_b, w_down, ln2_g, ln2_b):
    B, SEQ, D = x_prompt.shape
    NB, T, _ = x_sample.shape
    depth = w_in.shape[0]
    NM = meta_tokens.shape[0]
    H, DH = cache_k.shape[3], cache_k.shape[4]
    SH, P, N = state_ssm.shape[2], state_ssm.shape[3], state_ssm.shape[4]
    DI = SH * P
    XBC = state_conv_ssm.shape[-1]
    G = (XBC - DI) // (2 * N)
    dims = dict(B=B, SEQ=SEQ, NM=NM, NB=NB, T=T, D=D, H=H, DH=DH, AW=H * DH, DI=DI, XBC=XBC, SH=SH,
                P=P, N=N, G=G, E=SH // G, DFF=w_down.shape[1], alpha=(2.0 * depth) ** 0.25)
    assert SEQ % CH == 0 and NM <= LANES and (B * SEQ) % NM == 0 and H + SH <= LANES

    xs_small = jnp.concatenate([jnp.broadcast_to(meta_tokens[None], (B, NM, D)).reshape(B * NM, D),
                                x_sample.reshape(NB * T, D)], axis=0)
    tr = _row_tile(B * SEQ, 512)
    hf, hb = _ln_in(x_prompt.reshape(B * SEQ, D), xs_small, ln_in_g[None, :], ln_in_b[None, :], tr)

    sp, ss = [], []
    for l in range(depth):
        lw = (w_in[l], b_f[l], w_att_out[l], ssm_conv_w[l], ssm_conv_b[l], dt_bias[l], a_log[l], d_skip[l],
              ssm_norm_w[l], w_ssm_out[l], w_o[l], ln1_g[l], ln1_b[l], w_up[l], ffn_conv_w[l], ffn_conv_b[l],
              w_down[l], ln2_g[l], ln2_b[l])
        st = (cache_k[l], cache_v[l], cache_logf[l], state_conv_ssm[l], state_ssm[l], state_conv_ffn[l])
        hf, hb, st_p, st_s = _layer(hf, hb, lw, st, dims, page_table)
        sp.append(st_p)
        ss.append(st_s)

    stk = lambda lst, i: jnp.stack([s[i] for s in lst], axis=0)
    MB, MM = B * SEQ, B * NM
    y_prompt = hf[:MB].reshape(B, SEQ, D)
    y_sample = hf[MB + MM:].reshape(NB, T, D)
    return (y_prompt, y_sample, stk(sp, 0), stk(sp, 1), stk(sp, 2), stk(sp, 3), stk(sp, 4), stk(sp, 5),
            stk(ss, 0), stk(ss, 1), stk(ss, 2), stk(ss, 3), stk(ss, 4), stk(ss, 5))
```

```python
import functools

import jax
import jax.numpy as jnp
from jax import lax
from jax.experimental import pallas as pl
from jax.experimental.pallas import tpu as pltpu

F32 = jnp.float32
BF16 = jnp.bfloat16
LN_EPS = 1e-5
RMS_EPS = 1e-5
NEG = -1e30
LANES = 128
VMEM_LIMIT = 56 * 1024 * 1024


def _cp(*sem):
    return pltpu.CompilerParams(dimension_semantics=sem, vmem_limit_bytes=VMEM_LIMIT)


def _row_tile(m, target, mult=16):
    if m <= target:
        return m
    best = None
    for t in range(mult, target + 1, mult):
        if m % t == 0:
            best = t
    assert best is not None, (m, target)
    return best


def _dot(a, b):
    return jnp.dot(a, b, preferred_element_type=F32)


def _dot_nt(a, b):
    return lax.dot_general(a, b, (((1,), (1,)), ((), ())), preferred_element_type=F32)


def _dot_tn(a, b):
    return lax.dot_general(a, b, (((0,), (0,)), ((), ())), preferred_element_type=F32)


def _split3(x):
    hi = x.astype(BF16)
    r = x - hi.astype(F32)
    mid = r.astype(BF16)
    lo = (r - mid.astype(F32)).astype(BF16)
    return hi, mid, lo


def _sum01_left(m01, x):
    hi, mid, lo = _split3(x)
    return (_dot(m01, lo) + _dot(m01, mid)) + _dot(m01, hi)


def _sum01_right(x, m01):
    hi, mid, lo = _split3(x)
    return (_dot(lo, m01) + _dot(mid, m01)) + _dot(hi, m01)


def _softplus(x):
    return jnp.maximum(x, 0.0) + jnp.log1p(jnp.exp(-jnp.abs(x)))


def _silu(x):
    return x * (1.0 / (1.0 + jnp.exp(-x)))


def _sigmoid(x):
    return 1.0 / (1.0 + jnp.exp(-x))


def _ln_rows(x, g, b):
    mu = jnp.mean(x, -1, keepdims=True)
    xc = x - mu
    var = jnp.mean(xc * xc, -1, keepdims=True)
    return xc * lax.rsqrt(var + LN_EPS) * g + b


def _ln_in_kernel(xp_ref, xs_ref, g_ref, b_ref, hf_ref, hb_ref, *, n_big, m_small):
    i = pl.program_id(0)

    @pl.when(i < n_big)
    def _():
        y = _ln_rows(xp_ref[...], g_ref[...], b_ref[...])
        hf_ref[...] = y
        hb_ref[...] = y.astype(BF16)

    @pl.when(i == n_big)
    def _():
        y = _ln_rows(xs_ref[...], g_ref[...], b_ref[...])
        hf_ref[0:m_small, :] = y
        hb_ref[0:m_small, :] = y.astype(BF16)


def _ln_in(xp, xs, g, b, tr):
    mb, d = xp.shape
    ms = xs.shape[0]
    assert mb % tr == 0 and ms <= tr
    nb = mb // tr
    m = mb + ms
    return pl.pallas_call(
        functools.partial(_ln_in_kernel, n_big=nb, m_small=ms),
        grid=(nb + 1,),
        in_specs=[pl.BlockSpec((tr, d), lambda i: (jnp.minimum(i, nb - 1), 0)),
                  pl.BlockSpec((ms, d), lambda i: (0, 0)),
                  pl.BlockSpec((1, d), lambda i: (0, 0)),
                  pl.BlockSpec((1, d), lambda i: (0, 0))],
        out_specs=[pl.BlockSpec((tr, d), lambda i: (i, 0)),
                   pl.BlockSpec((tr, d), lambda i: (i, 0))],
        out_shape=[jax.ShapeDtypeStruct((m, d), F32), jax.ShapeDtypeStruct((m, d), BF16)],
        compiler_params=_cp("arbitrary"),
        name="ln_in",
    )(xp, xs, g, b)


def _mm_kernel(x_ref, w_ref, o_ref):
    o_ref[...] = _dot(x_ref[...], w_ref[...]).astype(o_ref.dtype)


def _matmul(x, w, tm, tn, out_dtype=F32, name="matmul"):
    m, k = x.shape
    n = w.shape[1]
    assert m % tm == 0 and n % tn == 0
    return pl.pallas_call(
        _mm_kernel,
        grid=(m // tm, n // tn),
        in_specs=[pl.BlockSpec((tm, k), lambda i, j: (i, 0)),
                  pl.BlockSpec((k, tn), lambda i, j: (0, j))],
        out_specs=pl.BlockSpec((tm, tn), lambda i, j: (i, j)),
        out_shape=jax.ShapeDtypeStruct((m, n), out_dtype),
        compiler_params=_cp("parallel", "arbitrary"),
        name=name,
    )(x, w)


def _small_proj_kernel(x_ref, w_ref, b_ref, o_ref, *, n_f):
    a = _dot(x_ref[...], w_ref[...]) + b_ref[...]
    lane = lax.broadcasted_iota(jnp.int32, a.shape, 1)
    sp_pos = _softplus(a)
    ls = -_softplus(-a)
    o_ref[...] = jnp.where(lane < n_f, ls, sp_pos)


def _small_proj(x, w, b, tm, n_f):
    m, k = x.shape
    return pl.pallas_call(
        functools.partial(_small_proj_kernel, n_f=n_f),
        grid=(m // tm,),
        in_specs=[pl.BlockSpec((tm, k), lambda i: (i, 0)),
                  pl.BlockSpec((k, LANES), lambda i: (0, 0)),
                  pl.BlockSpec((1, LANES), lambda i: (0, 0))],
        out_specs=pl.BlockSpec((tm, LANES), lambda i: (i, 0)),
        out_shape=jax.ShapeDtypeStruct((m, LANES), F32),
        compiler_params=_cp("arbitrary"),
        name="small_proj",
    )(x, w, b)


def _cumsum_kernel(x_ref, o_ref, *, nblk):
    h = x_ref.shape[0]
    r = lax.broadcasted_iota(jnp.int32, (LANES, LANES), 0)
    c = lax.broadcasted_iota(jnp.int32, (LANES, LANES), 1)
    tri = jnp.where(r <= c, 1.0, 0.0).astype(BF16)
    carry = jnp.zeros((h, 1), F32)
    for j in range(nblk):
        blk = _sum01_right(x_ref[:, j * LANES:(j + 1) * LANES], tri) + carry
        o_ref[:, j * LANES:(j + 1) * LANES] = blk
        carry = blk[:, LANES - 1:LANES]


def _cumsum_lanes(x):
    n, h, length = x.shape
    return pl.pallas_call(
        functools.partial(_cumsum_kernel, nblk=length // LANES),
        grid=(n,),
        in_specs=[pl.BlockSpec((None, h, length), lambda i: (i, 0, 0))],
        out_specs=pl.BlockSpec((None, h, length), lambda i: (i, 0, 0)),
        out_shape=jax.ShapeDtypeStruct((n, h, length), F32),
        compiler_params=_cp("arbitrary"),
        name="logf_cumsum",
    )(x)


def _bias_lanes(f, key_side):
    hi, mid, lo = _split3(-f if key_side else f)
    lane = lax.broadcasted_iota(jnp.int32, (f.shape[0], LANES), 1)
    f0, o0 = (3, 0) if key_side else (0, 3)
    v = jnp.where(lane == f0, hi.astype(F32), jnp.where(lane == f0 + 1, mid.astype(F32), lo.astype(F32)))
    v = jnp.where((lane >= f0) & (lane < f0 + 3), v, jnp.where((lane >= o0) & (lane < o0 + 3), 1.0, 0.0))
    return v.astype(BF16)


def _fox_prompt_kernel(qr_ref, kr_ref, vr_ref, qm_ref, km_ref, vm_ref, fc_ref,
                       or_ref, om_ref, k_s, v_s, qa_s, m_s, l_s, acc_s, *, nm, bq, scale):
    h = pl.program_id(1)
    seq, dh = qr_ref.shape
    pad = LANES - nm
    nq = seq // bq
    lpx = LANES + seq

    hsel = lax.broadcasted_iota(jnp.int32, (1, fc_ref.shape[1]), 1) == h

    def fcol(start, size):
        blk = fc_ref[pl.ds(start, size), :]
        return jnp.sum(jnp.where(hsel, blk, 0.0), axis=1, keepdims=True)

    k_s[0:pad, 0:dh] = jnp.zeros((pad, dh), BF16)
    v_s[0:pad, :] = jnp.zeros((pad, dh), BF16)
    k_s[pad:LANES, 0:dh] = km_ref[...].astype(BF16)
    v_s[pad:LANES, :] = vm_ref[...].astype(BF16)
    k_s[0:LANES, dh:] = _bias_lanes(fcol(0, LANES), True)
    cb = 512 if seq % 512 == 0 else LANES
    for r0 in range(0, seq, cb):
        k_s[LANES + r0:LANES + r0 + cb, 0:dh] = kr_ref[r0:r0 + cb, :].astype(BF16)
        k_s[LANES + r0:LANES + r0 + cb, dh:] = _bias_lanes(fcol(LANES + r0, cb), True)
        v_s[LANES + r0:LANES + r0 + cb, :] = vr_ref[r0:r0 + cb, :].astype(BF16)

    qm = jnp.concatenate([(qm_ref[...] * scale).astype(BF16), _bias_lanes(fcol(pad, nm), False)], axis=1)
    rm = lax.broadcasted_iota(jnp.int32, (nm, LANES), 0)
    cm = lax.broadcasted_iota(jnp.int32, (nm, LANES), 1)
    sm = jnp.where((cm >= pad) & (cm - pad <= rm), _dot_nt(qm, k_s[0:LANES, :]), NEG)
    pm = jnp.exp(sm - jnp.max(sm, axis=1, keepdims=True))
    om = _dot(pm.astype(BF16), v_s[0:LANES, :]) / jnp.sum(pm, axis=1, keepdims=True)
    om_ref[...] = om.astype(om_ref.dtype)

    rs = min(bq, 256)

    def update(r0, start, size, mask):
        s = _dot_nt(qa_s[r0:r0 + rs, :], k_s[pl.ds(start, size), :])
        if mask is not None:
            s = jnp.where(mask, s, NEG)
        m_old = m_s[r0:r0 + rs, :]
        m_new = jnp.maximum(m_old, jnp.max(s, axis=1, keepdims=True))
        alpha = jnp.exp(m_old - m_new)
        nl = size // LANES
        p = jnp.exp(s - (m_new if nl == 1 else jnp.concatenate([m_new] * nl, axis=1)))
        ps = p[:, 0:LANES]
        for u in range(1, nl):
            ps = ps + p[:, u * LANES:(u + 1) * LANES]
        l_s[r0:r0 + rs, :] = alpha * l_s[r0:r0 + rs, :] + ps
        acc_s[r0:r0 + rs, :] = alpha * acc_s[r0:r0 + rs, :] + _dot(p.astype(BF16), v_s[pl.ds(start, size), :])
        m_s[r0:r0 + rs, :] = m_new

    meta_cols = lax.broadcasted_iota(jnp.int32, (rs, LANES), 1) >= pad

    def qblock(i, _):
        q0 = pl.multiple_of(i * bq, bq)
        qa_s[:, 0:dh] = (qr_ref[pl.ds(q0, bq), :] * scale).astype(BF16)
        qa_s[:, dh:] = _bias_lanes(fcol(LANES + q0, bq), False)
        m_s[...] = jnp.full(m_s.shape, NEG, F32)
        l_s[...] = jnp.zeros(l_s.shape, F32)
        acc_s[...] = jnp.zeros(acc_s.shape, F32)
        for r0 in range(0, bq, rs):
            update(r0, 0, LANES, meta_cols)

        def kblock(j, c):
            for r0 in range(0, bq, rs):
                update(r0, pl.multiple_of(LANES + j * bq, LANES), bq, None)
            return c

        lax.fori_loop(0, i, kblock, 0)
        d0 = pl.multiple_of(LANES + q0, LANES)
        for r0 in range(0, bq, rs):
            size = r0 + rs
            rr = lax.broadcasted_iota(jnp.int32, (rs, size), 0)
            cc = lax.broadcasted_iota(jnp.int32, (rs, size), 1)
            update(r0, d0, size, cc <= rr + r0)
        l = jnp.sum(l_s[...], axis=1, keepdims=True)
        or_ref[pl.ds(q0, bq), :] = (acc_s[...] / l).astype(or_ref.dtype)
        return 0

    lax.fori_loop(0, nq, qblock, 0)


def _fox_prompt(p_main, f_c, *, n_b, seq, nm, n_h, dh, q_off, k_off, v_off, aw):
    bq = next(b for b in (1024, 512, 256, LANES) if seq % b == 0)
    assert dh == LANES
    qb, kb, vb = q_off // dh, k_off // dh, v_off // dh
    mrow = (n_b * seq) // nm
    lpx = LANES + seq
    real = lambda cb: pl.BlockSpec((seq, dh), lambda n, h: (n, cb + h))
    meta = lambda cb: pl.BlockSpec((nm, dh), lambda n, h: (mrow + n, cb + h))
    return pl.pallas_call(
        functools.partial(_fox_prompt_kernel, nm=nm, bq=bq, scale=dh ** -0.5),
        grid=(n_b, n_h),
        in_specs=[real(qb), real(kb), real(vb), meta(qb), meta(kb), meta(vb),
                  pl.BlockSpec((None, lpx, n_h), lambda n, h: (n, 0, 0))],
        out_specs=[pl.BlockSpec((seq, dh), lambda n, h: (n, h)),
                   pl.BlockSpec((nm, dh), lambda n, h: (n, h))],
        out_shape=[jax.ShapeDtypeStruct((p_main.shape[0], aw), BF16),
                   jax.ShapeDtypeStruct((n_b * nm, aw), BF16)],
        scratch_shapes=[pltpu.VMEM((lpx, 2 * dh), BF16), pltpu.VMEM((lpx, dh), BF16),
                        pltpu.VMEM((bq, 2 * dh), BF16), pltpu.VMEM((bq, LANES), F32),
                        pltpu.VMEM((bq, LANES), F32), pltpu.VMEM((bq, dh), F32)],
        compiler_params=_cp("parallel", "arbitrary"),
        name="fox_prompt",
    )(p_main, p_main, p_main, p_main, p_main, p_main, f_c)


def _fox_sample_kernel(pt_ref, q_ref, kn_ref, vn_ref, lfn_ref, *rest, n_h, pps):
    k_refs = rest[0:pps]
    v_refs = rest[pps:2 * pps]
    lf_refs = rest[2 * pps:3 * pps]
    o_ref = rest[3 * pps]
    q_s, mb_s, m_s, l_s, acc_s, car_s, kn_s, vn_s = rest[3 * pps + 1:]
    g = pl.program_id(1)
    ng = pl.num_programs(1)
    nq, dh = q_ref.shape
    nph = lf_refs[0].shape[0]

    ci = lax.broadcasted_iota(jnp.int32, (LANES, LANES), 0)
    cj = lax.broadcasted_iota(jnp.int32, (LANES, LANES), 1)
    same_h = (ci % n_h) == (cj % n_h)
    m12 = jnp.concatenate([jnp.where(same_h & (ci // n_h > cj // n_h), 1.0, 0.0),
                           jnp.where(same_h, 1.0, 0.0)], axis=1).astype(BF16)

    def suffix_rows(lf):
        r = lf.shape[0]
        hi, mid, lo = _split3(lf)
        res = _dot(jnp.concatenate([lo, mid, hi], axis=0), m12)
        res = (res[0:r] + res[r:2 * r]) + res[2 * r:3 * r]
        return res[:, 0:LANES], res[:, LANES:2 * LANES]

    def update(s, vb):
        m_old = m_s[...]
        m_new = jnp.maximum(m_old, jnp.max(s, axis=1, keepdims=True))
        alpha = jnp.exp(m_old - m_new)
        p = jnp.exp(s - m_new)
        l_s[...] = alpha * l_s[...] + jnp.sum(p, axis=1, keepdims=True)
        acc_s[...] = alpha * acc_s[...] + _dot(p.astype(BF16), vb)
        m_s[...] = m_new

    @pl.when(g == 0)
    def _():
        q_s[...] = (q_ref[...] * (dh ** -0.5)).astype(BF16)
        kn_s[...] = jnp.zeros(kn_s.shape, BF16)
        vn_s[...] = jnp.zeros(vn_s.shape, BF16)
        kn_s[0:nq, :] = kn_ref[...].astype(BF16)
        vn_s[0:nq, :] = vn_ref[...].astype(BF16)
        within, total = suffix_rows(lfn_ref[...])
        sfx = within[0:1, :]
        ri = lax.broadcasted_iota(jnp.int32, (nq, LANES), 0)
        li = lax.broadcasted_iota(jnp.int32, (nq, LANES), 1)
        rowc = -jnp.sum(jnp.where(ri == li, sfx, 0.0), axis=1, keepdims=True)
        head_ok = (ri % n_h) == (li % n_h)
        mb_s[...] = jnp.where(head_ok, rowc, NEG)
        m_s[...] = jnp.full(m_s.shape, NEG, F32)
        l_s[...] = jnp.zeros(l_s.shape, F32)
        acc_s[...] = jnp.zeros(acc_s.shape, F32)
        s = _dot_nt(q_s[...], kn_s[...]) + sfx + mb_s[...]
        s = jnp.where(li // n_h <= ri // n_h, s, NEG)
        update(s, vn_s[...])
        car_s[...] = total[0:1, :]

    q = q_s[...]
    mb = mb_s[...]
    prow = k_refs[0].shape[0]
    rows = lax.broadcasted_iota(jnp.int32, (nph, LANES), 0)
    within_all, total_all = suffix_rows(jnp.concatenate([lf_refs[j][...] for j in range(pps)], axis=0))
    car = car_s[...]
    s_parts = []
    for j in range(pps):
        total = total_all[j * nph:(j + 1) * nph]
        later = jnp.zeros((nph, LANES), F32)
        for r in range(1, nph):
            later = later + jnp.where(rows < r, total[r:r + 1, :], 0.0)
        bias = within_all[j * nph:(j + 1) * nph] + later + car
        car = car + jnp.sum(total, axis=0, keepdims=True)
        s = _dot_nt(q, k_refs[j][...].astype(BF16))
        s_parts += [s[:, r * LANES:(r + 1) * LANES] + (bias[r:r + 1, :] + mb) for r in range(nph)]
    car_s[...] = car
    s = jnp.concatenate(s_parts, axis=1)
    m_old = m_s[...]
    m_new = jnp.maximum(m_old, jnp.max(s, axis=1, keepdims=True))
    alpha = jnp.exp(m_old - m_new)
    p = jnp.exp(s - m_new)
    l_s[...] = alpha * l_s[...] + jnp.sum(p, axis=1, keepdims=True)
    pb = p.astype(BF16)
    pv = _dot(pb[:, 0:prow], v_refs[0][...].astype(BF16))
    for j in range(1, pps):
        pv = pv + _dot(pb[:, j * prow:(j + 1) * prow], v_refs[j][...].astype(BF16))
    acc_s[...] = alpha * acc_s[...] + pv
    m_s[...] = m_new

    @pl.when(g == ng - 1)
    def _():
        o_ref[...] = acc_s[...] / l_s[...]


def _fox_sample(q, kn, vn, lfn, ck, cv, clf, page_table, *, n_h, pps):
    nb, nq, dh = q.shape
    npg = page_table.shape[1]
    assert npg % pps == 0 and nq <= LANES and LANES % n_h == 0
    prow = ck.shape[1]
    nph = clf.shape[1]
    seq_spec = lambda r: pl.BlockSpec((None, r, dh), lambda b, g, pt: (b, 0, 0))

    def page_spec(rows, width, j):
        return pl.BlockSpec((None, rows, width), lambda b, g, pt: (pt[b, npg - 1 - (g * pps + j)], 0, 0))

    in_specs = [seq_spec(nq), seq_spec(nq), seq_spec(nq), pl.BlockSpec((None, 16, LANES), lambda b, g, pt: (b, 0, 0))]
    in_specs += [page_spec(prow, dh, j) for j in range(pps)]
    in_specs += [page_spec(prow, dh, j) for j in range(pps)]
    in_specs += [page_spec(nph, LANES, j) for j in range(pps)]
    grid_spec = pltpu.PrefetchScalarGridSpec(
        num_scalar_prefetch=1,
        grid=(nb, npg // pps),
        in_specs=in_specs,
        out_specs=pl.BlockSpec((None, nq, dh), lambda b, g, pt: (b, 0, 0)),
        scratch_shapes=[pltpu.VMEM((nq, dh), BF16), pltpu.VMEM((nq, LANES), F32),
                        pltpu.VMEM((nq, 1), F32), pltpu.VMEM((nq, 1), F32), pltpu.VMEM((nq, dh), F32),
                        pltpu.VMEM((1, LANES), F32),
                        pltpu.VMEM((LANES, dh), BF16), pltpu.VMEM((LANES, dh), BF16)],
    )
    return pl.pallas_call(
        functools.partial(_fox_sample_kernel, n_h=n_h, pps=pps),
        grid_spec=grid_spec,
        out_shape=jax.ShapeDtypeStruct((nb, nq, dh), F32),
        compiler_params=_cp("parallel", "arbitrary"),
        name="fox_sample",
    )(page_table, q, kn, vn, lfn, *([ck] * pps), *([cv] * pps), *([clf] * pps))


CH = 128


def _ssd_kernel(xs_ref, b_ref, c_ref, dtc_ref, dtr_ref, alc_ref, alr_ref, dsk_ref,
                wx_ref, wb_ref, wc_ref, bx_ref, bb_ref, bc_ref, px_ref, pb_ref, pc_ref, h0_ref,
                y_ref, hT_ref, ex_s, eb_s, ec_s, h_s, *, t_valid, n_e, p_dim):
    c = pl.program_id(2)
    nc = pl.num_programs(2)
    kw = wx_ref.shape[0]
    base = 8

    @pl.when(c == 0)
    def _():
        for e_s, p_ref in ((ex_s, px_ref), (eb_s, pb_ref), (ec_s, pc_ref)):
            e_s[...] = jnp.zeros(e_s.shape, F32)
            e_s[base - (kw - 1):base, :] = p_ref[...]
        h_s[...] = h0_ref[...].reshape(h_s.shape)

    def conv_silu(e_s, u_ref, w_ref, bias_ref):
        e_s[base:base + t_valid, :] = u_ref[...]
        out = bias_ref[...]
        for j in range(kw):
            out = out + e_s[base - (kw - 1) + j:base - (kw - 1) + j + CH, :] * w_ref[j:j + 1, :]
        return _silu(out)

    xc = conv_silu(ex_s, xs_ref, wx_ref, bx_ref)
    bm = conv_silu(eb_s, b_ref, wb_ref, bb_ref)
    cm = conv_silu(ec_s, c_ref, wc_ref, bc_ref)
    if t_valid == CH:
        for e_s in (ex_s, eb_s, ec_s):
            e_s[0:base, :] = e_s[CH:CH + base, :]

    row_c = lax.broadcasted_iota(jnp.int32, (CH, 1), 0)
    lane_r = lax.broadcasted_iota(jnp.int32, (1, CH), 1)
    dtc = dtc_ref[...]
    dtr = dtr_ref[...]
    if t_valid < CH:
        xc = jnp.where(row_c < t_valid, xc, 0.0)
    a_c = dtc * (-jnp.exp(alc_ref[...]))
    a_r = dtr * (-jnp.exp(alr_ref[...]))
    ri = lax.broadcasted_iota(jnp.int32, (CH, CH), 0)
    li = lax.broadcasted_iota(jnp.int32, (CH, CH), 1)
    causal = li <= ri
    tril = jnp.where(causal, 1.0, 0.0).astype(BF16)
    triu = jnp.where(ri <= li, 1.0, 0.0).astype(BF16)
    cum_c = _sum01_left(tril, a_c)
    cum_r = _sum01_right(a_r, triu)
    end_c = cum_c[CH - 1:CH, :]
    dd_c = jnp.exp(end_c - cum_c) * dtc
    ecum_c = jnp.exp(cum_c)
    cdec_r = jnp.exp(cum_r[:, CH - 1:CH])

    xb = xc.astype(BF16)
    bmb = bm.astype(BF16)
    cmb = cm.astype(BF16)
    cb = _dot_nt(cmb, bmb)
    lane2 = lax.broadcasted_iota(jnp.int32, (CH, 2 * p_dim), 1) < p_dim
    row2 = lax.broadcasted_iota(jnp.int32, (2 * p_dim, 1), 0) < p_dim
    ys = []
    for pr in range(n_e // 2):
        e0, e1 = 2 * pr, 2 * pr + 1
        xp = xb[:, e0 * p_dim:(e1 + 1) * p_dim]
        xpf = xc[:, e0 * p_dim:(e1 + 1) * p_dim]
        yd = []
        for e in (e0, e1):
            seg = cum_c[:, e:e + 1] - cum_r[e:e + 1, :]
            dec = jnp.exp(jnp.where(causal, seg, -jnp.inf))
            w = (cb * dec * dtr[e:e + 1, :]).astype(BF16)
            yd.append(_dot(w, xp))
        y_diag = jnp.where(lane2, yd[0], yd[1])
        hp = h_s[e0 * p_dim:(e1 + 1) * p_dim, :]
        y_off = _dot_nt(cmb, hp.astype(BF16)) * jnp.where(lane2, ecum_c[:, e0:e0 + 1], ecum_c[:, e1:e1 + 1])
        ys.append(y_diag + y_off + xpf * dsk_ref[:, e0 * p_dim:(e1 + 1) * p_dim])
        xw = (xpf * jnp.where(lane2, dd_c[:, e0:e0 + 1], dd_c[:, e1:e1 + 1])).astype(BF16)
        st = _dot_tn(xw, bmb)
        cd = jnp.where(row2, cdec_r[e0:e0 + 1, :], cdec_r[e1:e1 + 1, :])
        h_s[e0 * p_dim:(e1 + 1) * p_dim, :] = hp * cd + st
    y = jnp.concatenate(ys, axis=1)
    y_ref[...] = y[0:t_valid, :]

    @pl.when(c == nc - 1)
    def _():
        hT_ref[...] = h_s[...].reshape(hT_ref.shape)


def _ssd(xsrc, col_x, col_b, col_c, t_valid, n_seq, n_chunks, dtc, dtr, alc, alr, dsk,
         conv_w, conv_b, prev, h0, *, n_g, n_e, p_dim, n_state, d_inner, y_rows=None):
    gw = n_e * p_dim
    kw = conv_w.shape[0]
    three_d = xsrc.ndim == 3
    assert three_d or t_valid == CH
    row_of = lambda n, c: n * n_chunks + c

    def src_spec(width, colblk):
        if three_d:
            return pl.BlockSpec((None, t_valid, width), lambda n, g, c: (n, 0, colblk(g)))
        return pl.BlockSpec((CH, width), lambda n, g, c: (row_of(n, c), colblk(g)))

    cx = lambda g: col_x // gw + g
    cbk = lambda g: col_b // n_state + g
    cck = lambda g: col_c // n_state + g
    wx = lambda g: g
    wb = lambda g: d_inner // n_state + g
    wc = lambda g: (d_inner + n_g * n_state) // n_state + g

    if three_d:
        dtc_spec = pl.BlockSpec((None, None, CH, n_e), lambda n, g, c: (g, n, 0, 0))
        dtr_spec = pl.BlockSpec((None, n_e, CH), lambda n, g, c: (n, g, 0))
        y_spec = pl.BlockSpec((None, t_valid, gw), lambda n, g, c: (n, 0, g))
        y_shape = jax.ShapeDtypeStruct((n_seq, t_valid, d_inner), F32)
    else:
        dtc_spec = pl.BlockSpec((None, CH, n_e), lambda n, g, c: (g, row_of(n, c), 0))
        dtr_spec = pl.BlockSpec((n_e, CH), lambda n, g, c: (g, row_of(n, c)))
        y_spec = pl.BlockSpec((CH, gw), lambda n, g, c: (row_of(n, c), g))
        y_shape = jax.ShapeDtypeStruct((y_rows or n_seq * n_chunks * CH, d_inner), F32)

    in_specs = [
        src_spec(gw, cx), src_spec(n_state, cbk), src_spec(n_state, cck),
        dtc_spec, dtr_spec,
        pl.BlockSpec((None, 1, n_e), lambda n, g, c: (g, 0, 0)),
        pl.BlockSpec((n_e, 1), lambda n, g, c: (g, 0)),
        pl.BlockSpec((1, gw), lambda n, g, c: (0, g)),
        pl.BlockSpec((kw, gw), lambda n, g, c: (0, wx(g))),
        pl.BlockSpec((kw, n_state), lambda n, g, c: (0, wb(g))),
        pl.BlockSpec((kw, n_state), lambda n, g, c: (0, wc(g))),
        pl.BlockSpec((1, gw), lambda n, g, c: (0, wx(g))),
        pl.BlockSpec((1, n_state), lambda n, g, c: (0, wb(g))),
        pl.BlockSpec((1, n_state), lambda n, g, c: (0, wc(g))),
        pl.BlockSpec((None, kw - 1, gw), lambda n, g, c: (n, 0, wx(g))),
        pl.BlockSpec((None, kw - 1, n_state), lambda n, g, c: (n, 0, wb(g))),
        pl.BlockSpec((None, kw - 1, n_state), lambda n, g, c: (n, 0, wc(g))),
        pl.BlockSpec((None, n_e, p_dim, n_state), lambda n, g, c: (n, g, 0, 0)),
    ]
    out_specs = [y_spec, pl.BlockSpec((None, n_e, p_dim, n_state), lambda n, g, c: (n, g, 0, 0))]
    out_shape = [y_shape, jax.ShapeDtypeStruct((n_seq, n_g * n_e, p_dim, n_state), F32)]
    return pl.pallas_call(
        functools.partial(_ssd_kernel, t_valid=t_valid, n_e=n_e, p_dim=p_dim),
        grid=(n_seq, n_g, n_chunks),
        in_specs=in_specs, out_specs=out_specs, out_shape=out_shape,
        scratch_shapes=[pltpu.VMEM((CH + 8, gw), F32), pltpu.VMEM((CH + 8, n_state), F32),
                        pltpu.VMEM((CH + 8, n_state), F32), pltpu.VMEM((n_e * p_dim, n_state), F32)],
        compiler_params=_cp("parallel", "parallel", "arbitrary"),
        name="ssd",
    )(xsrc, xsrc, xsrc, dtc, dtr, alc, alr, dsk, conv_w, conv_w, conv_w, conv_b, conv_b, conv_b,
      prev, prev, prev, h0)


def _gated_norm_kernel(y_ref, z_ref, w_ref, o_ref):
    g = y_ref[...] * _silu(z_ref[...])
    ms = jnp.mean(g * g, -1, keepdims=True)
    o_ref[...] = (g * lax.rsqrt(ms + RMS_EPS) * w_ref[...]).astype(o_ref.dtype)


def _gated_norm(y, p_main, z_off, w, tm):
    m, di = y.shape
    assert z_off % di == 0
    zb = z_off // di
    return pl.pallas_call(
        _gated_norm_kernel,
        grid=(m // tm,),
        in_specs=[pl.BlockSpec((tm, di), lambda i: (i, 0)),
                  pl.BlockSpec((tm, di), lambda i: (i, zb)),
                  pl.BlockSpec((1, di), lambda i: (0, 0))],
        out_specs=pl.BlockSpec((tm, di), lambda i: (i, 0)),
        out_shape=jax.ShapeDtypeStruct((m, di), BF16),
        compiler_params=_cp("arbitrary"),
        name="gated_rmsnorm",
    )(y, p_main, w)


def _proj_gate_kernel(x_ref, w_ref, g_ref, *rest, add):
    if add:
        a_ref, o_ref = rest
    else:
        (o_ref,) = rest
    v = _sigmoid(g_ref[...]) * _dot(x_ref[...], w_ref[...])
    if add:
        v = v + a_ref[...]
    o_ref[...] = v.astype(o_ref.dtype)


def _proj_gate(x, w, p_main, g_off, addend, tm, tn, out_dtype):
    m, k = x.shape
    n = w.shape[1]
    assert g_off % tn == 0 and n % tn == 0
    gb = g_off // tn
    in_specs = [pl.BlockSpec((tm, k), lambda i, j: (i, 0)),
                pl.BlockSpec((k, tn), lambda i, j: (0, j)),
                pl.BlockSpec((tm, tn), lambda i, j: (i, gb + j))]
    args = [x, w, p_main]
    if addend is not None:
        in_specs.append(pl.BlockSpec((tm, tn), lambda i, j: (i, j)))
        args.append(addend)
    return pl.pallas_call(
        functools.partial(_proj_gate_kernel, add=addend is not None),
        grid=(m // tm, n // tn),
        in_specs=in_specs,
        out_specs=pl.BlockSpec((tm, tn), lambda i, j: (i, j)),
        out_shape=jax.ShapeDtypeStruct((m, n), out_dtype),
        compiler_params=_cp("parallel", "arbitrary"),
        name="proj_gate",
    )(*args)


def _mm_res_ln_kernel(x_ref, w_ref, r_ref, g_ref, b_ref, of_ref, ob_ref, acc_s, *, alpha):
    k = pl.program_id(1)

    @pl.when(k == 0)
    def _():
        acc_s[...] = alpha * r_ref[...]

    acc_s[...] += _dot(x_ref[...], w_ref[...])

    @pl.when(k == pl.num_programs(1) - 1)
    def _():
        y = _ln_rows(acc_s[...], g_ref[...], b_ref[...])
        of_ref[...] = y
        ob_ref[...] = y.astype(BF16)


def _mm_res_ln(x, w, res, g, b, alpha, tm, tk):
    m, k = x.shape
    n = w.shape[1]
    assert m % tm == 0 and k % tk == 0
    return pl.pallas_call(
        functools.partial(_mm_res_ln_kernel, alpha=alpha),
        grid=(m // tm, k // tk),
        in_specs=[pl.BlockSpec((tm, tk), lambda i, kk: (i, kk)),
                  pl.BlockSpec((tk, n), lambda i, kk: (kk, 0)),
                  pl.BlockSpec((tm, n), lambda i, kk: (i, 0)),
                  pl.BlockSpec((1, n), lambda i, kk: (0, 0)),
                  pl.BlockSpec((1, n), lambda i, kk: (0, 0))],
        out_specs=[pl.BlockSpec((tm, n), lambda i, kk: (i, 0)), pl.BlockSpec((tm, n), lambda i, kk: (i, 0))],
        out_shape=[jax.ShapeDtypeStruct((m, n), F32), jax.ShapeDtypeStruct((m, n), BF16)],
        scratch_shapes=[pltpu.VMEM((tm, n), F32)],
        compiler_params=_cp("parallel", "arbitrary"),
        name="mm_res_ln",
    )(x, w, res, g, b)


def _put_rows_kernel(dst_ref, src_ref, o_ref):
    del dst_ref
    o_ref[...] = src_ref[...]


def _put_rows(dst, src, row0, rb):
    n, w = src.shape
    assert row0 % rb == 0 and n % rb == 0 and row0 + n <= dst.shape[0]
    return pl.pallas_call(
        _put_rows_kernel,
        grid=(n // rb,),
        in_specs=[pl.BlockSpec(memory_space=pl.ANY), pl.BlockSpec((rb, w), lambda i: (i, 0))],
        out_specs=pl.BlockSpec((rb, w), lambda i: (row0 // rb + i, 0)),
        out_shape=jax.ShapeDtypeStruct(dst.shape, dst.dtype),
        input_output_aliases={0: 0},
        compiler_params=_cp("arbitrary"),
        name="put_rows",
    )(dst, src)


def _ffn_conv_small_kernel(*refs, kw):
    ua = refs[0:kw]
    ub = refs[kw:2 * kw]
    wa_ref, wb_ref, ba_ref, bb_ref, o_ref = refs[2 * kw:]
    a = ba_ref[...]
    b = bb_ref[...]
    for j in range(kw):
        a = a + ua[j][...] * wa_ref[j:j + 1, :]
        b = b + ub[j][...] * wb_ref[j:j + 1, :]
    o_ref[...] = (_silu(a) * b).astype(o_ref.dtype)


def _ffn_conv_small(taps, conv_w, conv_b, *, dff, tf):
    kw = conv_w.shape[0]
    rows = taps[0].shape[0]
    nj = dff // tf
    a_specs = [pl.BlockSpec((rows, tf), lambda j: (0, j)) for _ in range(kw)]
    b_specs = [pl.BlockSpec((rows, tf), lambda j: (0, nj + j)) for _ in range(kw)]
    return pl.pallas_call(
        functools.partial(_ffn_conv_small_kernel, kw=kw),
        grid=(nj,),
        in_specs=a_specs + b_specs + [pl.BlockSpec((kw, tf), lambda j: (0, j)),
                                      pl.BlockSpec((kw, tf), lambda j: (0, nj + j)),
                                      pl.BlockSpec((1, tf), lambda j: (0, j)),
                                      pl.BlockSpec((1, tf), lambda j: (0, nj + j))],
        out_specs=pl.BlockSpec((rows, tf), lambda j: (0, j)),
        out_shape=jax.ShapeDtypeStruct((rows, dff), BF16),
        compiler_params=_cp("arbitrary"),
        name="ffn_conv_small",
    )(*taps, *taps, conv_w, conv_w, conv_b, conv_b)


def _ffn_down_kernel(ua_ref, ub_ref, ha_ref, hb_ref, wa_ref, wb_ref, ba_ref, bb_ref, gs_ref, w_ref, r_ref,
                     g_ref, b_ref, o0_ref, o1_ref, acc_s, ea_s, eb_s, *, alpha, n_big, m_small, split):
    i = pl.program_id(0)
    k = pl.program_id(1)
    kw = wa_ref.shape[0]
    tm = ua_ref.shape[0]
    base = 8

    @pl.when(k == 0)
    def _():
        acc_s[...] = alpha * r_ref[...]

    def conv(e_s, u_ref, h_ref, w_ref, bias_ref):
        e_s[0:base, :] = h_ref[...]
        e_s[base:base + tm, :] = u_ref[...]
        out = bias_ref[...]
        for j in range(kw):
            out = out + e_s[base - (kw - 1) + j:base - (kw - 1) + j + tm, :] * w_ref[j:j + 1, :]
        return out

    @pl.when(i < n_big)
    def _():
        a = conv(ea_s, ua_ref, ha_ref, wa_ref, ba_ref)
        b = conv(eb_s, ub_ref, hb_ref, wb_ref, bb_ref)
        acc_s[...] += _dot((_silu(a) * b).astype(BF16), w_ref[...])

    @pl.when(i == n_big)
    def _():
        acc_s[0:m_small, :] += _dot(gs_ref[...], w_ref[...])

    @pl.when(k == pl.num_programs(1) - 1)
    def _():
        y = _ln_rows(acc_s[...], g_ref[...], b_ref[...])
        if not split:
            o0_ref[...] = y
            o1_ref[...] = y.astype(BF16)
        else:
            @pl.when(i < n_big)
            def _():
                o0_ref[...] = y

            @pl.when(i == n_big)
            def _():
                o1_ref[...] = y[0:m_small, :]


def _ffn_down(u, halo, g_small, conv_w, conv_b, w, res, g, b, alpha, *, m_big, dff, tm, tk, split):
    m = u.shape[0]
    n = w.shape[1]
    kw = conv_w.shape[0]
    nj = dff // tk
    m_small = m - m_big
    assert m_big % tm == 0 and 0 < m_small <= tm and dff % tk == 0
    n_big = m_big // tm
    big = lambda i: jnp.minimum(i, n_big - 1)
    if split:
        out_specs = [pl.BlockSpec((tm, n), lambda i, k: (big(i), 0)), pl.BlockSpec((m_small, n), lambda i, k: (0, 0))]
        out_shape = [jax.ShapeDtypeStruct((m_big, n), F32), jax.ShapeDtypeStruct((m_small, n), F32)]
    else:
        out_specs = [pl.BlockSpec((tm, n), lambda i, k: (i, 0)), pl.BlockSpec((tm, n), lambda i, k: (i, 0))]
        out_shape = [jax.ShapeDtypeStruct((m, n), F32), jax.ShapeDtypeStruct((m, n), BF16)]
    return pl.pallas_call(
        functools.partial(_ffn_down_kernel, alpha=alpha, n_big=n_big, m_small=m_small, split=split),
        grid=(n_big + 1, nj),
        in_specs=[pl.BlockSpec((tm, tk), lambda i, k: (big(i), k)),
                  pl.BlockSpec((tm, tk), lambda i, k: (big(i), nj + k)),
                  pl.BlockSpec((None, 8, tk), lambda i, k: (big(i), 0, k)),
                  pl.BlockSpec((None, 8, tk), lambda i, k: (big(i), 0, nj + k)),
                  pl.BlockSpec((kw, tk), lambda i, k: (0, k)),
                  pl.BlockSpec((kw, tk), lambda i, k: (0, nj + k)),
                  pl.BlockSpec((1, tk), lambda i, k: (0, k)),
                  pl.BlockSpec((1, tk), lambda i, k: (0, nj + k)),
                  pl.BlockSpec((m_small, tk), lambda i, k: (0, k)),
                  pl.BlockSpec((tk, n), lambda i, k: (k, 0)),
                  pl.BlockSpec((tm, n), lambda i, k: (i, 0)),
                  pl.BlockSpec((1, n), lambda i, k: (0, 0)),
                  pl.BlockSpec((1, n), lambda i, k: (0, 0))],
        out_specs=out_specs, out_shape=out_shape,
        scratch_shapes=[pltpu.VMEM((tm, n), F32), pltpu.VMEM((tm + 8, tk), F32), pltpu.VMEM((tm + 8, tk), F32)],
        compiler_params=_cp("arbitrary", "arbitrary"),
        name="ffn_down",
    )(u, u, halo, halo, conv_w, conv_w, conv_b, conv_b, g_small, w, res, g, b)


def _shift_taps(u_seq, prev):
    n, t, c = u_seq.shape
    k1 = prev.shape[1]
    full = jnp.concatenate([prev.astype(u_seq.dtype), u_seq], axis=1)
    return [full[:, j:j + t].reshape(n * t, c) for j in range(k1 + 1)]


def _layer(hf, hb, lw, st, dims, page_table, last):
    (w_in, b_f, w_att_out, ssm_conv_w, ssm_conv_b, dt_bias, a_log, d_skip, ssm_norm_w,
     w_ssm_out, w_o, ln1_g, ln1_b, w_up, ffn_conv_w, ffn_conv_b, w_down, ln2_g, ln2_b) = lw
    cache_k, cache_v, cache_logf, conv_ssm_s, ssm_s, conv_ffn_s = st
    d = dims
    B, SEQ, NM, NB, T, D = d["B"], d["SEQ"], d["NM"], d["NB"], d["T"], d["D"]
    H, DH, AW, DI, XBC, SH = d["H"], d["DH"], d["AW"], d["DI"], d["XBC"], d["SH"]
    P, N, G, E, DFF, alpha = d["P"], d["N"], d["G"], d["E"], d["DFF"], d["alpha"]
    M = hf.shape[0]
    MB, MM, MS = B * SEQ, B * NM, NB * T
    tm = _row_tile(M, 1392)
    tm_s = _row_tile(M, 512)

    o = [0]
    for s in (AW, AW, AW, H, DI, XBC, SH, D):
        o.append(o[-1] + s)
    wq, wk, wv, wf, wz, wx, wdt, wga, wgs = (w_in[:, o[i]:o[i] + s] for i, s in
                                              enumerate((AW, AW, AW, H, DI, XBC, SH, D, D)))
    w_main = jnp.concatenate([wz, wq, wk, wv, wx, wga, wgs], axis=1).astype(BF16)
    z_off, q_off, k_off, v_off = 0, DI, DI + AW, DI + 2 * AW
    x_off = DI + 3 * AW
    ga_off = x_off + XBC
    gs_off = ga_off + D
    nmain = gs_off + D
    w_small = jnp.concatenate([wf, wdt, jnp.zeros((D, LANES - H - SH), F32)], axis=1).astype(BF16)
    b_small = jnp.concatenate([b_f, dt_bias, jnp.zeros((LANES - H - SH,), F32)])[None, :]

    p_main = _matmul(hb, w_main, tm, 512, F32, name="in_proj")
    s_small = _small_proj(hb, w_small, b_small, tm, H)
    logf = s_small[:, :H]
    dt = s_small[:, H:H + SH]

    pad = LANES - NM
    lf_real = logf[:MB].reshape(B, SEQ, H)
    lf_meta = logf[MB:MB + MM].reshape(B, NM, H)
    lf_ext = jnp.concatenate([jnp.zeros((B, pad, H), F32), lf_meta, lf_real], axis=1)
    f_c = _cumsum_lanes(lf_ext.transpose(0, 2, 1)).transpose(0, 2, 1)
    att, o_meta = _fox_prompt(p_main, f_c, n_b=B, seq=SEQ, nm=NM, n_h=H, dh=DH,
                              q_off=q_off, k_off=k_off, v_off=v_off, aw=AW)
    rb = 32
    assert MB % rb == 0 and (MM + MS) % rb == 0

    p_s = p_main[MB + MM:]
    q_s = p_s[:, q_off:q_off + AW].reshape(NB, T * H, DH)
    k_s = p_s[:, k_off:k_off + AW].reshape(NB, T * H, DH)
    v_s = p_s[:, v_off:v_off + AW].reshape(NB, T * H, DH)
    lf_s = logf[MB + MM:].reshape(NB, T * H)
    lfn = jnp.zeros((NB, 16, LANES), F32).at[:, 0, :T * H].set(lf_s)
    pool, PAGE = cache_k.shape[0], cache_k.shape[1]
    ck = cache_k.reshape(pool, PAGE * H, DH)
    cv = cache_v.reshape(pool, PAGE * H, DH)
    clf = cache_logf.astype(F32).reshape(pool, (PAGE * H) // LANES, LANES)
    npg = page_table.shape[1]
    pps = 4 if npg % 4 == 0 else (2 if npg % 2 == 0 else 1)
    o_s = _fox_sample(q_s, k_s, v_s, lfn, ck, cv, clf, page_table, n_h=H, pps=pps)
    o_samp = o_s.reshape(NB * T, AW).astype(BF16)
    att = _put_rows(att, jnp.concatenate([o_meta, o_samp], axis=0), MB, rb)

    dt_c = dt.reshape(M, G, E).transpose(1, 0, 2)
    dt_r = dt.T
    alc = a_log.reshape(G, 1, E)
    alr = a_log.reshape(SH, 1)
    dsk = jnp.repeat(d_skip, P)[None, :]
    cbias = ssm_conv_b[None, :]
    kw = ssm_conv_w.shape[0]
    ssd = functools.partial(_ssd, n_g=G, n_e=E, p_dim=P, n_state=N, d_inner=DI)
    col_b = x_off + DI
    col_c = col_b + G * N
    zeros_prev = jnp.zeros((B, kw - 1, XBC), F32)
    zeros_h = jnp.zeros((B, SH, P, N), F32)

    def short_dt(lo, n_seq, t):
        c3 = dt_c[:, lo:lo + n_seq * t].reshape(G, n_seq, t, E)
        r3 = dt_r[:, lo:lo + n_seq * t].reshape(SH, n_seq, t).transpose(1, 0, 2)
        return (jnp.pad(c3, ((0, 0), (0, 0), (0, CH - t), (0, 0))), jnp.pad(r3, ((0, 0), (0, 0), (0, CH - t))))

    p_m3 = p_main[MB:MB + MM].reshape(B, NM, nmain)
    dt_c_m, dt_r_m = short_dt(MB, B, NM)
    y_meta, h_meta = ssd(p_m3, x_off, col_b, col_c, NM, B, 1, dt_c_m, dt_r_m, alc, alr, dsk,
                         ssm_conv_w, cbias, zeros_prev, zeros_h)
    prev_real = p_m3[:, NM - (kw - 1):, x_off:x_off + XBC]
    y_all, h_real = ssd(p_main, x_off, col_b, col_c, CH, B, SEQ // CH, dt_c, dt_r, alc, alr, dsk,
                        ssm_conv_w, cbias, prev_real, h_meta, y_rows=M)
    p_s3 = p_s.reshape(NB, T, nmain)
    dt_c_s, dt_r_s = short_dt(MB + MM, NB, T)
    y_samp, h_samp = ssd(p_s3, x_off, col_b, col_c, T, NB, 1, dt_c_s, dt_r_s, alc, alr, dsk,
                         ssm_conv_w, cbias, conv_ssm_s, ssm_s)
    y_small = jnp.concatenate([y_meta.reshape(MM, DI), y_samp.reshape(MS, DI)], axis=0)
    y_all = _put_rows(y_all, y_small, MB, rb)
    conv_ssm_p = jnp.stack([p_main[(n + 1) * SEQ - (kw - 1):(n + 1) * SEQ, x_off:x_off + XBC] for n in range(B)],
                           axis=0)
    xbc_s = p_s3[:, :, x_off:x_off + XBC]
    conv_ssm_new_s = jnp.concatenate([conv_ssm_s, xbc_s], axis=1)[:, T:]

    yn = _gated_norm(y_all, p_main, z_off, ssm_norm_w[None, :], tm_s)

    tn_o = 512 if D % 512 == 0 else D
    tm_p = _row_tile(M, 928)
    m1 = _proj_gate(att, w_att_out.astype(BF16), p_main, ga_off, None, tm_p, tn_o, F32)
    merged = _proj_gate(yn, w_ssm_out.astype(BF16), p_main, gs_off, m1, tm_p, tn_o, BF16)
    x1f, x1b = _mm_res_ln(merged, w_o.astype(BF16), hf, ln1_g[None, :], ln1_b[None, :], alpha, tm_s, 512)

    u = _matmul(x1b, w_up.astype(BF16), tm, 512, F32, name="ffn_up")
    kf = ffn_conv_w.shape[0]
    fbias = ffn_conv_b[None, :]
    tf = 512 if DFF % 512 == 0 else DFF
    u_meta = u[MB:MB + MM].reshape(B, NM, 2 * DFF)
    u_samp = u[MB + MM:].reshape(NB, T, 2 * DFF)
    taps_m = _shift_taps(u_meta, jnp.zeros((B, kf - 1, 2 * DFF), F32))
    taps_s = _shift_taps(u_samp, conv_ffn_s)
    taps = [jnp.concatenate([a, b], axis=0) for a, b in zip(taps_m, taps_s)]
    g_small = _ffn_conv_small(taps, ffn_conv_w, fbias, dff=DFF, tf=tf)
    tm_f = _row_tile(SEQ, 512)
    halo = jnp.stack([u_meta[r0 // SEQ, NM - 8:] if r0 % SEQ == 0 else u[r0 - 8:r0] for r0 in range(0, MB, tm_f)],
                     axis=0)
    x2 = _ffn_down(u, halo, g_small, ffn_conv_w, fbias, w_down.astype(BF16), x1f, ln2_g[None, :], ln2_b[None, :],
                   alpha, m_big=MB, dff=DFF, tm=tm_f, tk=tf, split=last)

    conv_ffn_p = jnp.stack([u[(n + 1) * SEQ - (kf - 1):(n + 1) * SEQ] for n in range(B)], axis=0)
    conv_ffn_new_s = jnp.concatenate([conv_ffn_s, u_samp], axis=1)[:, T:]

    def prompt_rows(col_off, width):
        real = p_main[:MB, col_off:col_off + width].reshape(B, SEQ, width)
        meta = p_main[MB:MB + MM, col_off:col_off + width].reshape(B, NM, width)
        return jnp.concatenate([meta, real], axis=1)

    k_p = prompt_rows(k_off, AW).reshape(B, NM + SEQ, H, DH)
    v_p = prompt_rows(v_off, AW).reshape(B, NM + SEQ, H, DH)
    lf_p = jnp.concatenate([lf_meta, lf_real], axis=1)
    k_sm = p_s[:, k_off:k_off + AW].reshape(NB, T, H, DH)
    v_sm = p_s[:, v_off:v_off + AW].reshape(NB, T, H, DH)
    lf_sm = logf[MB + MM:].reshape(NB, T, H)
    states_p = (k_p, v_p, lf_p, conv_ssm_p, h_real, conv_ffn_p)
    states_s = (k_sm, v_sm, lf_sm, conv_ssm_new_s, h_samp, conv_ffn_new_s)
    return x2, states_p, states_s


def kernel(x_prompt, x_sample, cache_k, cache_v, cache_logf, state_conv_ssm, state_ssm, state_conv_ffn, page_table, meta_tokens, ln_in_g, ln_in_b, w_in, b_f, w_att_out, ssm_conv_w, ssm_conv_b, dt_bias, a_log, d_skip, ssm_norm_w, w_ssm_out, w_o, ln1_g, ln1_b, w_up, ffn_conv_w, ffn_conv_b, w_down, ln2_g, ln2_b):
    B, SEQ, D = x_prompt.shape
    NB, T, _ = x_sample.shape
    depth = w_in.shape[0]
    NM = meta_tokens.shape[0]
    H, DH = cache_k.shape[3], cache_k.shape[4]
    SH, P, N = state_ssm.shape[2], state_ssm.shape[3], state_ssm.shape[4]
    DI = SH * P
    XBC = state_conv_ssm.shape[-1]
    G = (XBC - DI) // (2 * N)
    dims = dict(B=B, SEQ=SEQ, NM=NM, NB=NB, T=T, D=D, H=H, DH=DH, AW=H * DH, DI=DI, XBC=XBC, SH=SH,
                P=P, N=N, G=G, E=SH // G, DFF=w_down.shape[1], alpha=(2.0 * depth) ** 0.25)
    assert SEQ % CH == 0 and NM <= LANES and (B * SEQ) % NM == 0 and H + SH <= LANES

    xs_small = jnp.concatenate([jnp.broadcast_to(meta_tokens[None], (B, NM, D)).reshape(B * NM, D),
                                x_sample.reshape(NB * T, D)], axis=0)
    tr = _row_tile(B * SEQ, 512)
    hf, hb = _ln_in(x_prompt.reshape(B * SEQ, D), xs_small, ln_in_g[None, :], ln_in_b[None, :], tr)

    sp, ss = [], []
    for l in range(depth):
        lw = (w_in[l], b_f[l], w_att_out[l], ssm_conv_w[l], ssm_conv_b[l], dt_bias[l], a_log[l], d_skip[l],
              ssm_norm_w[l], w_ssm_out[l], w_o[l], ln1_g[l], ln1_b[l], w_up[l], ffn_conv_w[l], ffn_conv_b[l],
              w_down[l], ln2_g[l], ln2_b[l])
        st = (cache_k[l], cache_v[l], cache_logf[l], state_conv_ssm[l], state_ssm[l], state_conv_ffn[l])
        (hf, hb), st_p, st_s = _layer(hf, hb, lw, st, dims, page_table, l == depth - 1)
        sp.append(st_p)
        ss.append(st_s)

    stk = lambda lst, i: jnp.stack([s[i] for s in lst], axis=0)
    y_prompt = hf.reshape(B, SEQ, D)
    y_sample = hb[B * NM:].reshape(NB, T, D)
    return (y_prompt, y_sample, stk(sp, 0), stk(sp, 1), stk(sp, 2), stk(sp, 3), stk(sp, 4), stk(sp, 5),
            stk(ss, 0), stk(ss, 1), stk(ss, 2), stk(ss, 3), stk(ss, 4), stk(ss, 5))
```

```python
import functools

import jax
import jax.numpy as jnp
from jax import lax
from jax.experimental import pallas as pl
from jax.experimental.pallas import tpu as pltpu

F32 = jnp.float32
BF16 = jnp.bfloat16
LN_EPS = 1e-5
RMS_EPS = 1e-5
NEG = -1e30
LOG2E = 1.4426950408889634
LANES = 128
VMEM_LIMIT = 56 * 1024 * 1024


def _cp(*sem):
    return pltpu.CompilerParams(dimension_semantics=sem, vmem_limit_bytes=VMEM_LIMIT)


def _row_tile(m, target, mult=16):
    if m <= target:
        return m
    best = None
    for t in range(mult, target + 1, mult):
        if m % t == 0:
            best = t
    assert best is not None, (m, target)
    return best


def _dot(a, b):
    return jnp.dot(a, b, preferred_element_type=F32)


def _dot_nt(a, b):
    return lax.dot_general(a, b, (((1,), (1,)), ((), ())), preferred_element_type=F32)


def _dot_tn(a, b):
    return lax.dot_general(a, b, (((0,), (0,)), ((), ())), preferred_element_type=F32)


def _split3(x):
    hi = x.astype(BF16)
    r = x - hi.astype(F32)
    mid = r.astype(BF16)
    lo = (r - mid.astype(F32)).astype(BF16)
    return hi, mid, lo


def _sum01_left(m01, x):
    hi, mid, lo = _split3(x)
    return (_dot(m01, lo) + _dot(m01, mid)) + _dot(m01, hi)


def _sum01_right(x, m01):
    hi, mid, lo = _split3(x)
    return (_dot(lo, m01) + _dot(mid, m01)) + _dot(hi, m01)


def _softplus(x):
    return jnp.maximum(x, 0.0) + jnp.log1p(jnp.exp(-jnp.abs(x)))


def _silu(x):
    return x * (1.0 / (1.0 + jnp.exp(-x)))


def _sigmoid(x):
    return 1.0 / (1.0 + jnp.exp(-x))


def _ln_rows(x, g, b):
    mu = jnp.mean(x, -1, keepdims=True)
    xc = x - mu
    var = jnp.mean(xc * xc, -1, keepdims=True)
    return xc * lax.rsqrt(var + LN_EPS) * g + b


def _ln_in_kernel(xp_ref, xs_ref, g_ref, b_ref, hf_ref, hb_ref, *, n_big, m_small):
    i = pl.program_id(0)

    @pl.when(i < n_big)
    def _():
        y = _ln_rows(xp_ref[...], g_ref[...], b_ref[...])
        hf_ref[...] = y
        hb_ref[...] = y.astype(BF16)

    @pl.when(i == n_big)
    def _():
        y = _ln_rows(xs_ref[...], g_ref[...], b_ref[...])
        hf_ref[0:m_small, :] = y
        hb_ref[0:m_small, :] = y.astype(BF16)


def _ln_in(xp, xs, g, b, tr):
    mb, d = xp.shape
    ms = xs.shape[0]
    assert mb % tr == 0 and ms <= tr
    nb = mb // tr
    m = mb + ms
    return pl.pallas_call(
        functools.partial(_ln_in_kernel, n_big=nb, m_small=ms),
        grid=(nb + 1,),
        in_specs=[pl.BlockSpec((tr, d), lambda i: (jnp.minimum(i, nb - 1), 0)),
                  pl.BlockSpec((ms, d), lambda i: (0, 0)),
                  pl.BlockSpec((1, d), lambda i: (0, 0)),
                  pl.BlockSpec((1, d), lambda i: (0, 0))],
        out_specs=[pl.BlockSpec((tr, d), lambda i: (i, 0)),
                   pl.BlockSpec((tr, d), lambda i: (i, 0))],
        out_shape=[jax.ShapeDtypeStruct((m, d), F32), jax.ShapeDtypeStruct((m, d), BF16)],
        compiler_params=_cp("arbitrary"),
        name="ln_in",
    )(xp, xs, g, b)


def _mm_kernel(x_ref, w_ref, o_ref):
    o_ref[...] = _dot(x_ref[...], w_ref[...]).astype(o_ref.dtype)


def _matmul(x, w, tm, tn, out_dtype=F32, name="matmul"):
    m, k = x.shape
    n = w.shape[1]
    assert m % tm == 0 and n % tn == 0
    return pl.pallas_call(
        _mm_kernel,
        grid=(m // tm, n // tn),
        in_specs=[pl.BlockSpec((tm, k), lambda i, j: (i, 0)),
                  pl.BlockSpec((k, tn), lambda i, j: (0, j))],
        out_specs=pl.BlockSpec((tm, tn), lambda i, j: (i, j)),
        out_shape=jax.ShapeDtypeStruct((m, n), out_dtype),
        compiler_params=_cp("parallel", "arbitrary"),
        name=name,
    )(x, w)


def _small_proj_kernel(x_ref, w_ref, b_ref, o_ref, *, n_f):
    a = _dot(x_ref[...], w_ref[...]) + b_ref[...]
    lane = lax.broadcasted_iota(jnp.int32, a.shape, 1)
    sp_pos = _softplus(a)
    ls = -_softplus(-a)
    o_ref[...] = jnp.where(lane < n_f, ls, sp_pos)


def _small_proj(x, w, b, tm, n_f):
    m, k = x.shape
    return pl.pallas_call(
        functools.partial(_small_proj_kernel, n_f=n_f),
        grid=(m // tm,),
        in_specs=[pl.BlockSpec((tm, k), lambda i: (i, 0)),
                  pl.BlockSpec((k, LANES), lambda i: (0, 0)),
                  pl.BlockSpec((1, LANES), lambda i: (0, 0))],
        out_specs=pl.BlockSpec((tm, LANES), lambda i: (i, 0)),
        out_shape=jax.ShapeDtypeStruct((m, LANES), F32),
        compiler_params=_cp("arbitrary"),
        name="small_proj",
    )(x, w, b)


def _cumsum_kernel(x_ref, o_ref, *, nblk):
    h = x_ref.shape[0]
    r = lax.broadcasted_iota(jnp.int32, (LANES, LANES), 0)
    c = lax.broadcasted_iota(jnp.int32, (LANES, LANES), 1)
    tri = jnp.where(r <= c, 1.0, 0.0).astype(BF16)
    carry = jnp.zeros((h, 1), F32)
    for j in range(nblk):
        blk = _sum01_right(x_ref[:, j * LANES:(j + 1) * LANES], tri) + carry
        o_ref[:, j * LANES:(j + 1) * LANES] = blk
        carry = blk[:, LANES - 1:LANES]


def _cumsum_lanes(x):
    n, h, length = x.shape
    return pl.pallas_call(
        functools.partial(_cumsum_kernel, nblk=length // LANES),
        grid=(n,),
        in_specs=[pl.BlockSpec((None, h, length), lambda i: (i, 0, 0))],
        out_specs=pl.BlockSpec((None, h, length), lambda i: (i, 0, 0)),
        out_shape=jax.ShapeDtypeStruct((n, h, length), F32),
        compiler_params=_cp("arbitrary"),
        name="logf_cumsum",
    )(x)


def _bias_lanes(f, key_side):
    hi, mid, lo = _split3(-f if key_side else f)
    lane = lax.broadcasted_iota(jnp.int32, (f.shape[0], LANES), 1)
    f0, o0 = (3, 0) if key_side else (0, 3)
    v = jnp.where(lane == f0, hi.astype(F32), jnp.where(lane == f0 + 1, mid.astype(F32), lo.astype(F32)))
    v = jnp.where((lane >= f0) & (lane < f0 + 3), v, jnp.where((lane >= o0) & (lane < o0 + 3), 1.0, 0.0))
    return v.astype(BF16)


def _fox_prompt_kernel(qr_ref, kr_ref, vr_ref, qm_ref, km_ref, vm_ref, fc_ref,
                       or_ref, om_ref, k_s, v_s, qa_s, m_s, l_s, acc_s, *, nm, bq, scale):
    h = pl.program_id(1)
    seq, dh = qr_ref.shape
    pad = LANES - nm
    nq = seq // bq
    lpx = LANES + seq

    hsel = lax.broadcasted_iota(jnp.int32, (1, fc_ref.shape[1]), 1) == h

    def fcol(start, size):
        blk = fc_ref[pl.ds(start, size), :]
        return jnp.sum(jnp.where(hsel, blk, 0.0), axis=1, keepdims=True) * LOG2E

    k_s[0:pad, 0:dh] = jnp.zeros((pad, dh), BF16)
    v_s[0:pad, :] = jnp.zeros((pad, dh), BF16)
    k_s[pad:LANES, 0:dh] = km_ref[...].astype(BF16)
    v_s[pad:LANES, :] = vm_ref[...].astype(BF16)
    k_s[0:LANES, dh:] = _bias_lanes(fcol(0, LANES), True)
    cb = 512 if seq % 512 == 0 else LANES
    for r0 in range(0, seq, cb):
        k_s[LANES + r0:LANES + r0 + cb, 0:dh] = kr_ref[r0:r0 + cb, :].astype(BF16)
        k_s[LANES + r0:LANES + r0 + cb, dh:] = _bias_lanes(fcol(LANES + r0, cb), True)
        v_s[LANES + r0:LANES + r0 + cb, :] = vr_ref[r0:r0 + cb, :].astype(BF16)

    qm = jnp.concatenate([(qm_ref[...] * scale).astype(BF16), _bias_lanes(fcol(pad, nm), False)], axis=1)
    rm = lax.broadcasted_iota(jnp.int32, (nm, LANES), 0)
    cm = lax.broadcasted_iota(jnp.int32, (nm, LANES), 1)
    sm = jnp.where((cm >= pad) & (cm - pad <= rm), _dot_nt(qm, k_s[0:LANES, :]), NEG)
    pm = jnp.exp2(sm - jnp.max(sm, axis=1, keepdims=True))
    om = _dot(pm.astype(BF16), v_s[0:LANES, :]) / jnp.sum(pm, axis=1, keepdims=True)
    om_ref[...] = om.astype(om_ref.dtype)

    rs = min(bq, 256)

    def update(r0, start, size, mask):
        s = _dot_nt(qa_s[r0:r0 + rs, :], k_s[pl.ds(start, size), :])
        if mask is not None:
            s = jnp.where(mask, s, NEG)
        m_old = m_s[r0:r0 + rs, :]
        m_new = jnp.maximum(m_old, jnp.max(s, axis=1, keepdims=True))
        alpha = jnp.exp2(m_old - m_new)
        nl = size // LANES
        p = jnp.exp2(s - (m_new if nl == 1 else jnp.concatenate([m_new] * nl, axis=1)))
        ps = p[:, 0:LANES]
        for u in range(1, nl):
            ps = ps + p[:, u * LANES:(u + 1) * LANES]
        l_s[r0:r0 + rs, :] = alpha * l_s[r0:r0 + rs, :] + ps
        acc_s[r0:r0 + rs, :] = alpha * acc_s[r0:r0 + rs, :] + _dot(p.astype(BF16), v_s[pl.ds(start, size), :])
        m_s[r0:r0 + rs, :] = m_new

    meta_cols = lax.broadcasted_iota(jnp.int32, (rs, LANES), 1) >= pad

    def qblock(i, _):
        q0 = pl.multiple_of(i * bq, bq)
        qa_s[:, 0:dh] = (qr_ref[pl.ds(q0, bq), :] * scale).astype(BF16)
        qa_s[:, dh:] = _bias_lanes(fcol(LANES + q0, bq), False)
        m_s[...] = jnp.full(m_s.shape, NEG, F32)
        l_s[...] = jnp.zeros(l_s.shape, F32)
        acc_s[...] = jnp.zeros(acc_s.shape, F32)
        for r0 in range(0, bq, rs):
            update(r0, 0, LANES, meta_cols)

        def kblock(j, c):
            for r0 in range(0, bq, rs):
                update(r0, pl.multiple_of(LANES + j * bq, LANES), bq, None)
            return c

        lax.fori_loop(0, i, kblock, 0)
        d0 = pl.multiple_of(LANES + q0, LANES)
        for r0 in range(0, bq, rs):
            size = r0 + rs
            rr = lax.broadcasted_iota(jnp.int32, (rs, size), 0)
            cc = lax.broadcasted_iota(jnp.int32, (rs, size), 1)
            update(r0, d0, size, cc <= rr + r0)
        l = jnp.sum(l_s[...], axis=1, keepdims=True)
        or_ref[pl.ds(q0, bq), :] = (acc_s[...] / l).astype(or_ref.dtype)
        return 0

    lax.fori_loop(0, nq, qblock, 0)


def _fox_prompt(p_main, f_c, *, n_b, seq, nm, n_h, dh, q_off, k_off, v_off, aw):
    bq = next(b for b in (1024, 512, 256, LANES) if seq % b == 0)
    assert dh == LANES
    qb, kb, vb = q_off // dh, k_off // dh, v_off // dh
    mrow = (n_b * seq) // nm
    lpx = LANES + seq
    real = lambda cb: pl.BlockSpec((seq, dh), lambda n, h: (n, cb + h))
    meta = lambda cb: pl.BlockSpec((nm, dh), lambda n, h: (mrow + n, cb + h))
    return pl.pallas_call(
        functools.partial(_fox_prompt_kernel, nm=nm, bq=bq, scale=dh ** -0.5 * LOG2E),
        grid=(n_b, n_h),
        in_specs=[real(qb), real(kb), real(vb), meta(qb), meta(kb), meta(vb),
                  pl.BlockSpec((None, lpx, n_h), lambda n, h: (n, 0, 0))],
        out_specs=[pl.BlockSpec((seq, dh), lambda n, h: (n, h)),
                   pl.BlockSpec((nm, dh), lambda n, h: (n, h))],
        out_shape=[jax.ShapeDtypeStruct((p_main.shape[0], aw), BF16),
                   jax.ShapeDtypeStruct((n_b * nm, aw), BF16)],
        scratch_shapes=[pltpu.VMEM((lpx, 2 * dh), BF16), pltpu.VMEM((lpx, dh), BF16),
                        pltpu.VMEM((bq, 2 * dh), BF16), pltpu.VMEM((bq, LANES), F32),
                        pltpu.VMEM((bq, LANES), F32), pltpu.VMEM((bq, dh), F32)],
        compiler_params=_cp("parallel", "arbitrary"),
        name="fox_prompt",
    )(p_main, p_main, p_main, p_main, p_main, p_main, f_c)


def _fox_sample_kernel(pt_ref, q_ref, kn_ref, vn_ref, lfn_ref, *rest, n_h, pps):
    k_refs = rest[0:pps]
    v_refs = rest[pps:2 * pps]
    lf_refs = rest[2 * pps:3 * pps]
    o_ref = rest[3 * pps]
    q_s, mb_s, m_s, l_s, acc_s, car_s, kn_s, vn_s, m12_s = rest[3 * pps + 1:]
    g = pl.program_id(1)
    ng = pl.num_programs(1)
    nq, dh = q_ref.shape
    nph = lf_refs[0].shape[0]

    def suffix_rows(lf):
        r = lf.shape[0]
        hi, mid, lo = _split3(lf)
        res = _dot(jnp.concatenate([lo, mid, hi], axis=0), m12_s[...])
        res = (res[0:r] + res[r:2 * r]) + res[2 * r:3 * r]
        return res[:, 0:LANES], res[:, LANES:2 * LANES]

    def update(s, vb):
        m_old = m_s[...]
        m_new = jnp.maximum(m_old, jnp.max(s, axis=1, keepdims=True))
        alpha = jnp.exp2(m_old - m_new)
        p = jnp.exp2(s - m_new)
        l_s[...] = alpha * l_s[...] + jnp.sum(p, axis=1, keepdims=True)
        acc_s[...] = alpha * acc_s[...] + _dot(p.astype(BF16), vb)
        m_s[...] = m_new

    @pl.when(g == 0)
    def _():
        ci = lax.broadcasted_iota(jnp.int32, (LANES, LANES), 0)
        cj = lax.broadcasted_iota(jnp.int32, (LANES, LANES), 1)
        same_h = (ci % n_h) == (cj % n_h)
        m12_s[...] = jnp.concatenate([jnp.where(same_h & (ci // n_h > cj // n_h), 1.0, 0.0),
                                      jnp.where(same_h, 1.0, 0.0)], axis=1).astype(BF16)
        q_s[...] = (q_ref[...] * (dh ** -0.5 * LOG2E)).astype(BF16)
        kn_s[...] = jnp.zeros(kn_s.shape, BF16)
        vn_s[...] = jnp.zeros(vn_s.shape, BF16)
        kn_s[0:nq, :] = kn_ref[...].astype(BF16)
        vn_s[0:nq, :] = vn_ref[...].astype(BF16)
        within, total = suffix_rows(lfn_ref[...])
        sfx = within[0:1, :] * LOG2E
        ri = lax.broadcasted_iota(jnp.int32, (nq, LANES), 0)
        li = lax.broadcasted_iota(jnp.int32, (nq, LANES), 1)
        rowc = -jnp.sum(jnp.where(ri == li, sfx, 0.0), axis=1, keepdims=True)
        head_ok = (ri % n_h) == (li % n_h)
        mb_s[...] = jnp.where(head_ok, rowc, NEG)
        m_s[...] = jnp.full(m_s.shape, NEG, F32)
        l_s[...] = jnp.zeros(l_s.shape, F32)
        acc_s[...] = jnp.zeros(acc_s.shape, F32)
        s = _dot_nt(q_s[...], kn_s[...]) + sfx + mb_s[...]
        s = jnp.where(li // n_h <= ri // n_h, s, NEG)
        update(s, vn_s[...])
        car_s[...] = total[0:1, :]

    q = q_s[...]
    mb = mb_s[...]
    prow = k_refs[0].shape[0]
    rows = lax.broadcasted_iota(jnp.int32, (nph, LANES), 0)
    within_all, total_all = suffix_rows(jnp.concatenate([lf_refs[j][...] for j in range(pps)], axis=0))
    car = car_s[...]
    s_parts = []
    for j in range(pps):
        total = total_all[j * nph:(j + 1) * nph]
        later = jnp.zeros((nph, LANES), F32)
        for r in range(1, nph):
            later = later + jnp.where(rows < r, total[r:r + 1, :], 0.0)
        bias = (within_all[j * nph:(j + 1) * nph] + later + car) * LOG2E
        car = car + jnp.sum(total, axis=0, keepdims=True)
        s = _dot_nt(q, k_refs[j][...].astype(BF16))
        s_parts += [s[:, r * LANES:(r + 1) * LANES] + (bias[r:r + 1, :] + mb) for r in range(nph)]
    car_s[...] = car
    s = jnp.concatenate(s_parts, axis=1)
    m_old = m_s[...]
    m_new = jnp.maximum(m_old, jnp.max(s, axis=1, keepdims=True))
    alpha = jnp.exp2(m_old - m_new)
    p = jnp.exp2(s - m_new)
    l_s[...] = alpha * l_s[...] + jnp.sum(p, axis=1, keepdims=True)
    pb = p.astype(BF16)
    pv = _dot(pb[:, 0:prow], v_refs[0][...].astype(BF16))
    for j in range(1, pps):
        pv = pv + _dot(pb[:, j * prow:(j + 1) * prow], v_refs[j][...].astype(BF16))
    acc_s[...] = alpha * acc_s[...] + pv
    m_s[...] = m_new

    @pl.when(g == ng - 1)
    def _():
        o_ref[...] = acc_s[...] / l_s[...]


def _fox_sample(q, kn, vn, lfn, ck, cv, clf, page_table, *, n_h, pps):
    nb, nq, dh = q.shape
    npg = page_table.shape[1]
    assert npg % pps == 0 and nq <= LANES and LANES % n_h == 0
    prow = ck.shape[1]
    nph = clf.shape[1]
    seq_spec = lambda r: pl.BlockSpec((None, r, dh), lambda b, g, pt: (b, 0, 0))

    def page_spec(rows, width, j):
        return pl.BlockSpec((None, rows, width), lambda b, g, pt: (pt[b, npg - 1 - (g * pps + j)], 0, 0))

    in_specs = [seq_spec(nq), seq_spec(nq), seq_spec(nq), pl.BlockSpec((None, 16, LANES), lambda b, g, pt: (b, 0, 0))]
    in_specs += [page_spec(prow, dh, j) for j in range(pps)]
    in_specs += [page_spec(prow, dh, j) for j in range(pps)]
    in_specs += [page_spec(nph, LANES, j) for j in range(pps)]
    grid_spec = pltpu.PrefetchScalarGridSpec(
        num_scalar_prefetch=1,
        grid=(nb, npg // pps),
        in_specs=in_specs,
        out_specs=pl.BlockSpec((None, nq, dh), lambda b, g, pt: (b, 0, 0)),
        scratch_shapes=[pltpu.VMEM((nq, dh), BF16), pltpu.VMEM((nq, LANES), F32),
                        pltpu.VMEM((nq, 1), F32), pltpu.VMEM((nq, 1), F32), pltpu.VMEM((nq, dh), F32),
                        pltpu.VMEM((1, LANES), F32),
                        pltpu.VMEM((LANES, dh), BF16), pltpu.VMEM((LANES, dh), BF16),
                        pltpu.VMEM((LANES, 2 * LANES), BF16)],
    )
    return pl.pallas_call(
        functools.partial(_fox_sample_kernel, n_h=n_h, pps=pps),
        grid_spec=grid_spec,
        out_shape=jax.ShapeDtypeStruct((nb, nq, dh), F32),
        compiler_params=_cp("parallel", "arbitrary"),
        name="fox_sample",
    )(page_table, q, kn, vn, lfn, *([ck] * pps), *([cv] * pps), *([clf] * pps))


CH = 128


def _ssd_kernel(xs_ref, b_ref, c_ref, dtc_ref, dtr_ref, alc_ref, alr_ref, dsk_ref,
                wx_ref, wb_ref, wc_ref, bx_ref, bb_ref, bc_ref, px_ref, pb_ref, pc_ref, h0_ref,
                y_ref, hT_ref, ex_s, eb_s, ec_s, h_s, *, t_valid, n_sub, n_e, p_dim):
    c = pl.program_id(2)
    nc = pl.num_programs(2)
    kw = wx_ref.shape[0]
    base = 8
    rows = CH * n_sub
    rows_in = t_valid if t_valid < CH else rows

    @pl.when(c == 0)
    def _():
        for e_s, p_ref in ((ex_s, px_ref), (eb_s, pb_ref), (ec_s, pc_ref)):
            e_s[...] = jnp.zeros(e_s.shape, F32)
            e_s[base - (kw - 1):base, :] = p_ref[...]
        h_s[...] = h0_ref[...].reshape(h_s.shape)

    def conv_silu(e_s, u_ref, w_ref, bias_ref):
        e_s[base:base + rows_in, :] = u_ref[...]
        out = bias_ref[...]
        for j in range(kw):
            out = out + e_s[base - (kw - 1) + j:base - (kw - 1) + j + rows, :] * w_ref[j:j + 1, :]
        return _silu(out)

    xc_all = conv_silu(ex_s, xs_ref, wx_ref, bx_ref)
    bm_all = conv_silu(eb_s, b_ref, wb_ref, bb_ref)
    cm_all = conv_silu(ec_s, c_ref, wc_ref, bc_ref)
    if t_valid == CH:
        for e_s in (ex_s, eb_s, ec_s):
            e_s[0:base, :] = e_s[rows:rows + base, :]
    else:
        row_c = lax.broadcasted_iota(jnp.int32, (CH, 1), 0)
        xc_all = jnp.where(row_c < t_valid, xc_all, 0.0)

    neg_a_c = -jnp.exp(alc_ref[...])
    neg_a_r = -jnp.exp(alr_ref[...])
    ri = lax.broadcasted_iota(jnp.int32, (CH, CH), 0)
    li = lax.broadcasted_iota(jnp.int32, (CH, CH), 1)
    causal = li <= ri
    tril = jnp.where(causal, 1.0, 0.0).astype(BF16)
    triu = jnp.where(ri <= li, 1.0, 0.0).astype(BF16)
    lane2 = lax.broadcasted_iota(jnp.int32, (CH, 2 * p_dim), 1) < p_dim
    row2 = lax.broadcasted_iota(jnp.int32, (2 * p_dim, 1), 0) < p_dim

    for sub in range(n_sub):
        r0 = sub * CH
        xc = xc_all[r0:r0 + CH]
        dtc = dtc_ref[r0:r0 + CH, :]
        dtr = dtr_ref[:, r0:r0 + CH]
        cum_c = _sum01_left(tril, dtc * neg_a_c) * LOG2E
        cum_r = _sum01_right(dtr * neg_a_r, triu) * LOG2E
        end_c = cum_c[CH - 1:CH, :]
        dd_c = jnp.exp2(end_c - cum_c) * dtc
        ecum_c = jnp.exp2(cum_c)
        cdec_r = jnp.exp2(cum_r[:, CH - 1:CH])

        xb = xc.astype(BF16)
        bmb = bm_all[r0:r0 + CH].astype(BF16)
        cmb = cm_all[r0:r0 + CH].astype(BF16)
        cb = _dot_nt(cmb, bmb)
        ys = []
        for pr in range(n_e // 2):
            e0, e1 = 2 * pr, 2 * pr + 1
            xp = xb[:, e0 * p_dim:(e1 + 1) * p_dim]
            xpf = xc[:, e0 * p_dim:(e1 + 1) * p_dim]
            yd = []
            for e in (e0, e1):
                seg = cum_c[:, e:e + 1] - cum_r[e:e + 1, :]
                dec = jnp.exp2(jnp.where(causal, seg, -jnp.inf))
                w = (cb * dec * dtr[e:e + 1, :]).astype(BF16)
                yd.append(_dot(w, xp))
            y_diag = jnp.where(lane2, yd[0], yd[1])
            hp = h_s[e0 * p_dim:(e1 + 1) * p_dim, :]
            y_off = _dot_nt(cmb, hp.astype(BF16)) * jnp.where(lane2, ecum_c[:, e0:e0 + 1], ecum_c[:, e1:e1 + 1])
            ys.append(y_diag + y_off + xpf * dsk_ref[:, e0 * p_dim:(e1 + 1) * p_dim])
            xw = (xpf * jnp.where(lane2, dd_c[:, e0:e0 + 1], dd_c[:, e1:e1 + 1])).astype(BF16)
            st = _dot_tn(xw, bmb)
            cd = jnp.where(row2, cdec_r[e0:e0 + 1, :], cdec_r[e1:e1 + 1, :])
            h_s[e0 * p_dim:(e1 + 1) * p_dim, :] = hp * cd + st
        y = jnp.concatenate(ys, axis=1)
        if t_valid < CH:
            y_ref[...] = y[0:t_valid, :]
        else:
            y_ref[r0:r0 + CH, :] = y

    @pl.when(c == nc - 1)
    def _():
        hT_ref[...] = h_s[...].reshape(hT_ref.shape)


def _ssd(xsrc, col_x, col_b, col_c, t_valid, n_seq, n_chunks, dtc, dtr, alc, alr, dsk,
         conv_w, conv_b, prev, h0, *, n_g, n_e, p_dim, n_state, d_inner, y_rows=None):
    gw = n_e * p_dim
    kw = conv_w.shape[0]
    three_d = xsrc.ndim == 3
    assert three_d or t_valid == CH
    n_sub = 2 if (not three_d and n_chunks % 2 == 0) else 1
    n_steps = n_chunks // n_sub
    rows = CH * n_sub
    row_of = lambda n, c: n * n_steps + c

    def src_spec(width, colblk):
        if three_d:
            return pl.BlockSpec((None, t_valid, width), lambda n, g, c: (n, 0, colblk(g)))
        return pl.BlockSpec((rows, width), lambda n, g, c: (row_of(n, c), colblk(g)))

    cx = lambda g: col_x // gw + g
    cbk = lambda g: col_b // n_state + g
    cck = lambda g: col_c // n_state + g
    wx = lambda g: g
    wb = lambda g: d_inner // n_state + g
    wc = lambda g: (d_inner + n_g * n_state) // n_state + g

    if three_d:
        dtc_spec = pl.BlockSpec((None, None, CH, n_e), lambda n, g, c: (g, n, 0, 0))
        dtr_spec = pl.BlockSpec((None, n_e, CH), lambda n, g, c: (n, g, 0))
        y_spec = pl.BlockSpec((None, t_valid, gw), lambda n, g, c: (n, 0, g))
        y_shape = jax.ShapeDtypeStruct((n_seq, t_valid, d_inner), F32)
    else:
        dtc_spec = pl.BlockSpec((None, rows, n_e), lambda n, g, c: (g, row_of(n, c), 0))
        dtr_spec = pl.BlockSpec((n_e, rows), lambda n, g, c: (g, row_of(n, c)))
        y_spec = pl.BlockSpec((rows, gw), lambda n, g, c: (row_of(n, c), g))
        y_shape = jax.ShapeDtypeStruct((y_rows or n_seq * n_chunks * CH, d_inner), F32)

    in_specs = [
        src_spec(gw, cx), src_spec(n_state, cbk), src_spec(n_state, cck),
        dtc_spec, dtr_spec,
        pl.BlockSpec((None, 1, n_e), lambda n, g, c: (g, 0, 0)),
        pl.BlockSpec((n_e, 1), lambda n, g, c: (g, 0)),
        pl.BlockSpec((1, gw), lambda n, g, c: (0, g)),
        pl.BlockSpec((kw, gw), lambda n, g, c: (0, wx(g))),
        pl.BlockSpec((kw, n_state), lambda n, g, c: (0, wb(g))),
        pl.BlockSpec((kw, n_state), lambda n, g, c: (0, wc(g))),
        pl.BlockSpec((1, gw), lambda n, g, c: (0, wx(g))),
        pl.BlockSpec((1, n_state), lambda n, g, c: (0, wb(g))),
        pl.BlockSpec((1, n_state), lambda n, g, c: (0, wc(g))),
        pl.BlockSpec((None, kw - 1, gw), lambda n, g, c: (n, 0, wx(g))),
        pl.BlockSpec((None, kw - 1, n_state), lambda n, g, c: (n, 0, wb(g))),
        pl.BlockSpec((None, kw - 1, n_state), lambda n, g, c: (n, 0, wc(g))),
        pl.BlockSpec((None, n_e, p_dim, n_state), lambda n, g, c: (n, g, 0, 0)),
    ]
    out_specs = [y_spec, pl.BlockSpec((None, n_e, p_dim, n_state), lambda n, g, c: (n, g, 0, 0))]
    out_shape = [y_shape, jax.ShapeDtypeStruct((n_seq, n_g * n_e, p_dim, n_state), F32)]
    return pl.pallas_call(
        functools.partial(_ssd_kernel, t_valid=t_valid, n_sub=n_sub, n_e=n_e, p_dim=p_dim),
        grid=(n_seq, n_g, n_steps),
        in_specs=in_specs, out_specs=out_specs, out_shape=out_shape,
        scratch_shapes=[pltpu.VMEM((rows + 8, gw), F32), pltpu.VMEM((rows + 8, n_state), F32),
                        pltpu.VMEM((rows + 8, n_state), F32), pltpu.VMEM((n_e * p_dim, n_state), F32)],
        compiler_params=_cp("parallel", "parallel", "arbitrary"),
        name="ssd",
    )(xsrc, xsrc, xsrc, dtc, dtr, alc, alr, dsk, conv_w, conv_w, conv_w, conv_b, conv_b, conv_b,
      prev, prev, prev, h0)


def _gated_norm_kernel(y_ref, z_ref, w_ref, o_ref):
    g = y_ref[...] * _silu(z_ref[...])
    ms = jnp.mean(g * g, -1, keepdims=True)
    o_ref[...] = (g * lax.rsqrt(ms + RMS_EPS) * w_ref[...]).astype(o_ref.dtype)


def _gated_norm(y, p_main, z_off, w, tm):
    m, di = y.shape
    assert z_off % di == 0
    zb = z_off // di
    return pl.pallas_call(
        _gated_norm_kernel,
        grid=(m // tm,),
        in_specs=[pl.BlockSpec((tm, di), lambda i: (i, 0)),
                  pl.BlockSpec((tm, di), lambda i: (i, zb)),
                  pl.BlockSpec((1, di), lambda i: (0, 0))],
        out_specs=pl.BlockSpec((tm, di), lambda i: (i, 0)),
        out_shape=jax.ShapeDtypeStruct((m, di), BF16),
        compiler_params=_cp("arbitrary"),
        name="gated_rmsnorm",
    )(y, p_main, w)


def _proj_gate_kernel(x_ref, w_ref, g_ref, *rest, add):
    if add:
        a_ref, o_ref = rest
    else:
        (o_ref,) = rest
    v = _sigmoid(g_ref[...]) * _dot(x_ref[...], w_ref[...])
    if add:
        v = v + a_ref[...]
    o_ref[...] = v.astype(o_ref.dtype)


def _proj_gate(x, w, p_main, g_off, addend, tm, tn, out_dtype):
    m, k = x.shape
    n = w.shape[1]
    assert g_off % tn == 0 and n % tn == 0
    gb = g_off // tn
    in_specs = [pl.BlockSpec((tm, k), lambda i, j: (i, 0)),
                pl.BlockSpec((k, tn), lambda i, j: (0, j)),
                pl.BlockSpec((tm, tn), lambda i, j: (i, gb + j))]
    args = [x, w, p_main]
    if addend is not None:
        in_specs.append(pl.BlockSpec((tm, tn), lambda i, j: (i, j)))
        args.append(addend)
    return pl.pallas_call(
        functools.partial(_proj_gate_kernel, add=addend is not None),
        grid=(m // tm, n // tn),
        in_specs=in_specs,
        out_specs=pl.BlockSpec((tm, tn), lambda i, j: (i, j)),
        out_shape=jax.ShapeDtypeStruct((m, n), out_dtype),
        compiler_params=_cp("parallel", "arbitrary"),
        name="proj_gate",
    )(*args)


def _mm_res_ln_kernel(x_ref, w_ref, r_ref, g_ref, b_ref, of_ref, ob_ref, acc_s, *, alpha):
    k = pl.program_id(1)

    @pl.when(k == 0)
    def _():
        acc_s[...] = alpha * r_ref[...]

    acc_s[...] += _dot(x_ref[...], w_ref[...])

    @pl.when(k == pl.num_programs(1) - 1)
    def _():
        y = _ln_rows(acc_s[...], g_ref[...], b_ref[...])
        of_ref[...] = y
        ob_ref[...] = y.astype(BF16)


def _mm_res_ln(x, w, res, g, b, alpha, tm, tk):
    m, k = x.shape
    n = w.shape[1]
    assert m % tm == 0 and k % tk == 0
    return pl.pallas_call(
        functools.partial(_mm_res_ln_kernel, alpha=alpha),
        grid=(m // tm, k // tk),
        in_specs=[pl.BlockSpec((tm, tk), lambda i, kk: (i, kk)),
                  pl.BlockSpec((tk, n), lambda i, kk: (kk, 0)),
                  pl.BlockSpec((tm, n), lambda i, kk: (i, 0)),
                  pl.BlockSpec((1, n), lambda i, kk: (0, 0)),
                  pl.BlockSpec((1, n), lambda i, kk: (0, 0))],
        out_specs=[pl.BlockSpec((tm, n), lambda i, kk: (i, 0)), pl.BlockSpec((tm, n), lambda i, kk: (i, 0))],
        out_shape=[jax.ShapeDtypeStruct((m, n), F32), jax.ShapeDtypeStruct((m, n), BF16)],
        scratch_shapes=[pltpu.VMEM((tm, n), F32)],
        compiler_params=_cp("parallel", "arbitrary"),
        name="mm_res_ln",
    )(x, w, res, g, b)


def _put_rows_kernel(dst_ref, src_ref, o_ref):
    del dst_ref
    o_ref[...] = src_ref[...]


def _put_rows(dst, src, row0, rb):
    n, w = src.shape
    assert row0 % rb == 0 and n % rb == 0 and row0 + n <= dst.shape[0]
    return pl.pallas_call(
        _put_rows_kernel,
        grid=(n // rb,),
        in_specs=[pl.BlockSpec(memory_space=pl.ANY), pl.BlockSpec((rb, w), lambda i: (i, 0))],
        out_specs=pl.BlockSpec((rb, w), lambda i: (row0 // rb + i, 0)),
        out_shape=jax.ShapeDtypeStruct(dst.shape, dst.dtype),
        input_output_aliases={0: 0},
        compiler_params=_cp("arbitrary"),
        name="put_rows",
    )(dst, src)


def _ffn_conv_small_kernel(*refs, kw):
    ua = refs[0:kw]
    ub = refs[kw:2 * kw]
    wa_ref, wb_ref, ba_ref, bb_ref, o_ref = refs[2 * kw:]
    a = ba_ref[...]
    b = bb_ref[...]
    for j in range(kw):
        a = a + ua[j][...] * wa_ref[j:j + 1, :]
        b = b + ub[j][...] * wb_ref[j:j + 1, :]
    o_ref[...] = (_silu(a) * b).astype(o_ref.dtype)


def _ffn_conv_small(taps, conv_w, conv_b, *, dff, tf):
    kw = conv_w.shape[0]
    rows = taps[0].shape[0]
    nj = dff // tf
    a_specs = [pl.BlockSpec((rows, tf), lambda j: (0, j)) for _ in range(kw)]
    b_specs = [pl.BlockSpec((rows, tf), lambda j: (0, nj + j)) for _ in range(kw)]
    return pl.pallas_call(
        functools.partial(_ffn_conv_small_kernel, kw=kw),
        grid=(nj,),
        in_specs=a_specs + b_specs + [pl.BlockSpec((kw, tf), lambda j: (0, j)),
                                      pl.BlockSpec((kw, tf), lambda j: (0, nj + j)),
                                      pl.BlockSpec((1, tf), lambda j: (0, j)),
                                      pl.BlockSpec((1, tf), lambda j: (0, nj + j))],
        out_specs=pl.BlockSpec((rows, tf), lambda j: (0, j)),
        out_shape=jax.ShapeDtypeStruct((rows, dff), BF16),
        compiler_params=_cp("arbitrary"),
        name="ffn_conv_small",
    )(*taps, *taps, conv_w, conv_w, conv_b, conv_b)


def _ffn_down_kernel(ua_ref, ub_ref, ha_ref, hb_ref, wa_ref, wb_ref, ba_ref, bb_ref, gs_ref, w_ref, r_ref,
                     g_ref, b_ref, o0_ref, o1_ref, acc_s, ea_s, eb_s, *, alpha, n_big, m_small, split):
    i = pl.program_id(0)
    k = pl.program_id(1)
    kw = wa_ref.shape[0]
    tm = ua_ref.shape[0]
    base = 8

    @pl.when(k == 0)
    def _():
        acc_s[...] = alpha * r_ref[...]

    def conv(e_s, u_ref, h_ref, w_ref, bias_ref):
        e_s[0:base, :] = h_ref[...]
        e_s[base:base + tm, :] = u_ref[...]
        out = bias_ref[...]
        for j in range(kw):
            out = out + e_s[base - (kw - 1) + j:base - (kw - 1) + j + tm, :] * w_ref[j:j + 1, :]
        return out

    @pl.when(i < n_big)
    def _():
        a = conv(ea_s, ua_ref, ha_ref, wa_ref, ba_ref)
        b = conv(eb_s, ub_ref, hb_ref, wb_ref, bb_ref)
        acc_s[...] += _dot((_silu(a) * b).astype(BF16), w_ref[...])

    @pl.when(i == n_big)
    def _():
        acc_s[0:m_small, :] += _dot(gs_ref[...], w_ref[...])

    @pl.when(k == pl.num_programs(1) - 1)
    def _():
        y = _ln_rows(acc_s[...], g_ref[...], b_ref[...])
        if not split:
            o0_ref[...] = y
            o1_ref[...] = y.astype(BF16)
        else:
            @pl.when(i < n_big)
            def _():
                o0_ref[...] = y

            @pl.when(i == n_big)
            def _():
                o1_ref[...] = y[0:m_small, :]


def _ffn_down(u, halo, g_small, conv_w, conv_b, w, res, g, b, alpha, *, m_big, dff, tm, tk, split):
    m = u.shape[0]
    n = w.shape[1]
    kw = conv_w.shape[0]
    nj = dff // tk
    m_small = m - m_big
    assert m_big % tm == 0 and 0 < m_small <= tm and dff % tk == 0
    n_big = m_big // tm
    big = lambda i: jnp.minimum(i, n_big - 1)
    if split:
        out_specs = [pl.BlockSpec((tm, n), lambda i, k: (big(i), 0)), pl.BlockSpec((m_small, n), lambda i, k: (0, 0))]
        out_shape = [jax.ShapeDtypeStruct((m_big, n), F32), jax.ShapeDtypeStruct((m_small, n), F32)]
    else:
        out_specs = [pl.BlockSpec((tm, n), lambda i, k: (i, 0)), pl.BlockSpec((tm, n), lambda i, k: (i, 0))]
        out_shape = [jax.ShapeDtypeStruct((m, n), F32), jax.ShapeDtypeStruct((m, n), BF16)]
    return pl.pallas_call(
        functools.partial(_ffn_down_kernel, alpha=alpha, n_big=n_big, m_small=m_small, split=split),
        grid=(n_big + 1, nj),
        in_specs=[pl.BlockSpec((tm, tk), lambda i, k: (big(i), k)),
                  pl.BlockSpec((tm, tk), lambda i, k: (big(i), nj + k)),
                  pl.BlockSpec((None, 8, tk), lambda i, k: (big(i), 0, k)),
                  pl.BlockSpec((None, 8, tk), lambda i, k: (big(i), 0, nj + k)),
                  pl.BlockSpec((kw, tk), lambda i, k: (0, k)),
                  pl.BlockSpec((kw, tk), lambda i, k: (0, nj + k)),
                  pl.BlockSpec((1, tk), lambda i, k: (0, k)),
                  pl.BlockSpec((1, tk), lambda i, k: (0, nj + k)),
                  pl.BlockSpec((m_small, tk), lambda i, k: (0, k)),
                  pl.BlockSpec((tk, n), lambda i, k: (k, 0)),
                  pl.BlockSpec((tm, n), lambda i, k: (i, 0)),
                  pl.BlockSpec((1, n), lambda i, k: (0, 0)),
                  pl.BlockSpec((1, n), lambda i, k: (0, 0))],
        out_specs=out_specs, out_shape=out_shape,
        scratch_shapes=[pltpu.VMEM((tm, n), F32), pltpu.VMEM((tm + 8, tk), F32), pltpu.VMEM((tm + 8, tk), F32)],
        compiler_params=_cp("arbitrary", "arbitrary"),
        name="ffn_down",
    )(u, u, halo, halo, conv_w, conv_w, conv_b, conv_b, g_small, w, res, g, b)


def _shift_taps(u_seq, prev):
    n, t, c = u_seq.shape
    k1 = prev.shape[1]
    full = jnp.concatenate([prev.astype(u_seq.dtype), u_seq], axis=1)
    return [full[:, j:j + t].reshape(n * t, c) for j in range(k1 + 1)]


def _layer(hf, hb, lw, st, dims, page_table, last):
    (w_in, b_f, w_att_out, ssm_conv_w, ssm_conv_b, dt_bias, a_log, d_skip, ssm_norm_w,
     w_ssm_out, w_o, ln1_g, ln1_b, w_up, ffn_conv_w, ffn_conv_b, w_down, ln2_g, ln2_b) = lw
    cache_k, cache_v, cache_logf, conv_ssm_s, ssm_s, conv_ffn_s = st
    d = dims
    B, SEQ, NM, NB, T, D = d["B"], d["SEQ"], d["NM"], d["NB"], d["T"], d["D"]
    H, DH, AW, DI, XBC, SH = d["H"], d["DH"], d["AW"], d["DI"], d["XBC"], d["SH"]
    P, N, G, E, DFF, alpha = d["P"], d["N"], d["G"], d["E"], d["DFF"], d["alpha"]
    M = hf.shape[0]
    MB, MM, MS = B * SEQ, B * NM, NB * T
    tm = _row_tile(M, 1392)
    tm_s = _row_tile(M, 512)

    o = [0]
    for s in (AW, AW, AW, H, DI, XBC, SH, D):
        o.append(o[-1] + s)
    wq, wk, wv, wf, wz, wx, wdt, wga, wgs = (w_in[:, o[i]:o[i] + s] for i, s in
                                              enumerate((AW, AW, AW, H, DI, XBC, SH, D, D)))
    w_main = jnp.concatenate([wz, wq, wk, wv, wx, wga, wgs], axis=1).astype(BF16)
    z_off, q_off, k_off, v_off = 0, DI, DI + AW, DI + 2 * AW
    x_off = DI + 3 * AW
    ga_off = x_off + XBC
    gs_off = ga_off + D
    nmain = gs_off + D
    w_small = jnp.concatenate([wf, wdt, jnp.zeros((D, LANES - H - SH), F32)], axis=1).astype(BF16)
    b_small = jnp.concatenate([b_f, dt_bias, jnp.zeros((LANES - H - SH,), F32)])[None, :]

    p_main = _matmul(hb, w_main, tm, 512, F32, name="in_proj")
    s_small = _small_proj(hb, w_small, b_small, tm, H)
    logf = s_small[:, :H]
    dt = s_small[:, H:H + SH]

    pad = LANES - NM
    lf_real = logf[:MB].reshape(B, SEQ, H)
    lf_meta = logf[MB:MB + MM].reshape(B, NM, H)
    lf_ext = jnp.concatenate([jnp.zeros((B, pad, H), F32), lf_meta, lf_real], axis=1)
    f_c = _cumsum_lanes(lf_ext.transpose(0, 2, 1)).transpose(0, 2, 1)
    att, o_meta = _fox_prompt(p_main, f_c, n_b=B, seq=SEQ, nm=NM, n_h=H, dh=DH,
                              q_off=q_off, k_off=k_off, v_off=v_off, aw=AW)
    rb = 32
    assert MB % rb == 0 and (MM + MS) % rb == 0

    p_s = p_main[MB + MM:]
    q_s = p_s[:, q_off:q_off + AW].reshape(NB, T * H, DH)
    k_s = p_s[:, k_off:k_off + AW].reshape(NB, T * H, DH)
    v_s = p_s[:, v_off:v_off + AW].reshape(NB, T * H, DH)
    lf_s = logf[MB + MM:].reshape(NB, T * H)
    lfn = jnp.zeros((NB, 16, LANES), F32).at[:, 0, :T * H].set(lf_s)
    pool, PAGE = cache_k.shape[0], cache_k.shape[1]
    ck = cache_k.reshape(pool, PAGE * H, DH)
    cv = cache_v.reshape(pool, PAGE * H, DH)
    clf = cache_logf.astype(F32).reshape(pool, (PAGE * H) // LANES, LANES)
    npg = page_table.shape[1]
    pps = next(p for p in (8, 4, 2, 1) if npg % p == 0)
    o_s = _fox_sample(q_s, k_s, v_s, lfn, ck, cv, clf, page_table, n_h=H, pps=pps)
    o_samp = o_s.reshape(NB * T, AW).astype(BF16)
    att = _put_rows(att, jnp.concatenate([o_meta, o_samp], axis=0), MB, rb)

    dt_c = dt.reshape(M, G, E).transpose(1, 0, 2)
    dt_r = dt.T
    alc = a_log.reshape(G, 1, E)
    alr = a_log.reshape(SH, 1)
    dsk = jnp.repeat(d_skip, P)[None, :]
    cbias = ssm_conv_b[None, :]
    kw = ssm_conv_w.shape[0]
    ssd = functools.partial(_ssd, n_g=G, n_e=E, p_dim=P, n_state=N, d_inner=DI)
    col_b = x_off + DI
    col_c = col_b + G * N
    zeros_prev = jnp.zeros((B, kw - 1, XBC), F32)
    zeros_h = jnp.zeros((B, SH, P, N), F32)

    def short_dt(lo, n_seq, t):
        c3 = dt_c[:, lo:lo + n_seq * t].reshape(G, n_seq, t, E)
        r3 = dt_r[:, lo:lo + n_seq * t].reshape(SH, n_seq, t).transpose(1, 0, 2)
        return (jnp.pad(c3, ((0, 0), (0, 0), (0, CH - t), (0, 0))), jnp.pad(r3, ((0, 0), (0, 0), (0, CH - t))))

    p_m3 = p_main[MB:MB + MM].reshape(B, NM, nmain)
    dt_c_m, dt_r_m = short_dt(MB, B, NM)
    y_meta, h_meta = ssd(p_m3, x_off, col_b, col_c, NM, B, 1, dt_c_m, dt_r_m, alc, alr, dsk,
                         ssm_conv_w, cbias, zeros_prev, zeros_h)
    prev_real = p_m3[:, NM - (kw - 1):, x_off:x_off + XBC]
    y_all, h_real = ssd(p_main, x_off, col_b, col_c, CH, B, SEQ // CH, dt_c, dt_r, alc, alr, dsk,
                        ssm_conv_w, cbias, prev_real, h_meta, y_rows=M)
    p_s3 = p_s.reshape(NB, T, nmain)
    dt_c_s, dt_r_s = short_dt(MB + MM, NB, T)
    y_samp, h_samp = ssd(p_s3, x_off, col_b, col_c, T, NB, 1, dt_c_s, dt_r_s, alc, alr, dsk,
                         ssm_conv_w, cbias, conv_ssm_s, ssm_s)
    y_small = jnp.concatenate([y_meta.reshape(MM, DI), y_samp.reshape(MS, DI)], axis=0)
    y_all = _put_rows(y_all, y_small, MB, rb)
    conv_ssm_p = jnp.stack([p_main[(n + 1) * SEQ - (kw - 1):(n + 1) * SEQ, x_off:x_off + XBC] for n in range(B)],
                           axis=0)
    xbc_s = p_s3[:, :, x_off:x_off + XBC]
    conv_ssm_new_s = jnp.concatenate([conv_ssm_s, xbc_s], axis=1)[:, T:]

    yn = _gated_norm(y_all, p_main, z_off, ssm_norm_w[None, :], tm_s)

    tn_o = 512 if D % 512 == 0 else D
    tm_p = _row_tile(M, 928)
    m1 = _proj_gate(att, w_att_out.astype(BF16), p_main, ga_off, None, tm_p, tn_o, F32)
    merged = _proj_gate(yn, w_ssm_out.astype(BF16), p_main, gs_off, m1, tm_p, tn_o, BF16)
    x1f, x1b = _mm_res_ln(merged, w_o.astype(BF16), hf, ln1_g[None, :], ln1_b[None, :], alpha, tm_s,
                          D if D <= 2048 else 512)

    u = _matmul(x1b, w_up.astype(BF16), tm, 512, F32, name="ffn_up")
    kf = ffn_conv_w.shape[0]
    fbias = ffn_conv_b[None, :]
    tf = 512 if DFF % 512 == 0 else DFF
    u_meta = u[MB:MB + MM].reshape(B, NM, 2 * DFF)
    u_samp = u[MB + MM:].reshape(NB, T, 2 * DFF)
    taps_m = _shift_taps(u_meta, jnp.zeros((B, kf - 1, 2 * DFF), F32))
    taps_s = _shift_taps(u_samp, conv_ffn_s)
    taps = [jnp.concatenate([a, b], axis=0) for a, b in zip(taps_m, taps_s)]
    g_small = _ffn_conv_small(taps, ffn_conv_w, fbias, dff=DFF, tf=tf)
    tm_f = _row_tile(SEQ, 512)
    halo = jnp.stack([u_meta[r0 // SEQ, NM - 8:] if r0 % SEQ == 0 else u[r0 - 8:r0] for r0 in range(0, MB, tm_f)],
                     axis=0)
    x2 = _ffn_down(u, halo, g_small, ffn_conv_w, fbias, w_down.astype(BF16), x1f, ln2_g[None, :], ln2_b[None, :],
                   alpha, m_big=MB, dff=DFF, tm=tm_f, tk=tf, split=last)

    conv_ffn_p = jnp.stack([u[(n + 1) * SEQ - (kf - 1):(n + 1) * SEQ] for n in range(B)], axis=0)
    conv_ffn_new_s = jnp.concatenate([conv_ffn_s, u_samp], axis=1)[:, T:]

    def prompt_rows(col_off, width):
        real = p_main[:MB, col_off:col_off + width].reshape(B, SEQ, width)
        meta = p_main[MB:MB + MM, col_off:col_off + width].reshape(B, NM, width)
        return jnp.concatenate([meta, real], axis=1)

    k_p = prompt_rows(k_off, AW).reshape(B, NM + SEQ, H, DH)
    v_p = prompt_rows(v_off, AW).reshape(B, NM + SEQ, H, DH)
    lf_p = jnp.concatenate([lf_meta, lf_real], axis=1)
    k_sm = p_s[:, k_off:k_off + AW].reshape(NB, T, H, DH)
    v_sm = p_s[:, v_off:v_off + AW].reshape(NB, T, H, DH)
    lf_sm = logf[MB + MM:].reshape(NB, T, H)
    states_p = (k_p, v_p, lf_p, conv_ssm_p, h_real, conv_ffn_p)
    states_s = (k_sm, v_sm, lf_sm, conv_ssm_new_s, h_samp, conv_ffn_new_s)
    return x2, states_p, states_s


def kernel(x_prompt, x_sample, cache_k, cache_v, cache_logf, state_conv_ssm, state_ssm, state_conv_ffn, page_table, meta_tokens, ln_in_g, ln_in_b, w_in, b_f, w_att_out, ssm_conv_w, ssm_conv_b, dt_bias, a_log, d_skip, ssm_norm_w, w_ssm_out, w_o, ln1_g, ln1_b, w_up, ffn_conv_w, ffn_conv_b, w_down, ln2_g, ln2_b):
    B, SEQ, D = x_prompt.shape
    NB, T, _ = x_sample.shape
    depth = w_in.shape[0]
    NM = meta_tokens.shape[0]
    H, DH = cache_k.shape[3], cache_k.shape[4]
    SH, P, N = state_ssm.shape[2], state_ssm.shape[3], state_ssm.shape[4]
    DI = SH * P
    XBC = state_conv_ssm.shape[-1]
    G = (XBC - DI) // (2 * N)
    dims = dict(B=B, SEQ=SEQ, NM=NM, NB=NB, T=T, D=D, H=H, DH=DH, AW=H * DH, DI=DI, XBC=XBC, SH=SH,
                P=P, N=N, G=G, E=SH // G, DFF=w_down.shape[1], alpha=(2.0 * depth) ** 0.25)
    assert SEQ % CH == 0 and NM <= LANES and (B * SEQ) % NM == 0 and H + SH <= LANES

    xs_small = jnp.concatenate([jnp.broadcast_to(meta_tokens[None], (B, NM, D)).reshape(B * NM, D),
                                x_sample.reshape(NB * T, D)], axis=0)
    tr = _row_tile(B * SEQ, 512)
    hf, hb = _ln_in(x_prompt.reshape(B * SEQ, D), xs_small, ln_in_g[None, :], ln_in_b[None, :], tr)

    sp, ss = [], []
    for l in range(depth):
        lw = (w_in[l], b_f[l], w_att_out[l], ssm_conv_w[l], ssm_conv_b[l], dt_bias[l], a_log[l], d_skip[l],
              ssm_norm_w[l], w_ssm_out[l], w_o[l], ln1_g[l], ln1_b[l], w_up[l], ffn_conv_w[l], ffn_conv_b[l],
              w_down[l], ln2_g[l], ln2_b[l])
        st = (cache_k[l], cache_v[l], cache_logf[l], state_conv_ssm[l], state_ssm[l], state_conv_ffn[l])
        (hf, hb), st_p, st_s = _layer(hf, hb, lw, st, dims, page_table, l == depth - 1)
        sp.append(st_p)
        ss.append(st_s)

    stk = lambda lst, i: jnp.stack([s[i] for s in lst], axis=0)
    y_prompt = hf.reshape(B, SEQ, D)
    y_sample = hb[B * NM:].reshape(NB, T, D)
    return (y_prompt, y_sample, stk(sp, 0), stk(sp, 1), stk(sp, 2), stk(sp, 3), stk(sp, 4), stk(sp, 5),
            stk(ss, 0), stk(ss, 1), stk(ss, 2), stk(ss, 3), stk(ss, 4), stk(ss, 5))
```

```python
import functools

import jax
import jax.numpy as jnp
from jax import lax
from jax.experimental import pallas as pl
from jax.experimental.pallas import tpu as pltpu

F32 = jnp.float32
BF16 = jnp.bfloat16
LN_EPS = 1e-5
RMS_EPS = 1e-5
NEG = -1e30
LOG2E = 1.4426950408889634
LANES = 128
VMEM_LIMIT = 56 * 1024 * 1024


def _cp(*sem):
    return pltpu.CompilerParams(dimension_semantics=sem, vmem_limit_bytes=VMEM_LIMIT)


def _row_tile(m, target, mult=16):
    if m <= target:
        return m
    best = None
    for t in range(mult, target + 1, mult):
        if m % t == 0:
            best = t
    assert best is not None, (m, target)
    return best


def _dot(a, b):
    return jnp.dot(a, b, preferred_element_type=F32)


def _dot_nt(a, b):
    return lax.dot_general(a, b, (((1,), (1,)), ((), ())), preferred_element_type=F32)


def _dot_tn(a, b):
    return lax.dot_general(a, b, (((0,), (0,)), ((), ())), preferred_element_type=F32)


def _split3(x):
    hi = x.astype(BF16)
    r = x - hi.astype(F32)
    mid = r.astype(BF16)
    lo = (r - mid.astype(F32)).astype(BF16)
    return hi, mid, lo


def _sum01_left(m01, x):
    hi, mid, lo = _split3(x)
    return (_dot(m01, lo) + _dot(m01, mid)) + _dot(m01, hi)


def _sum01_right(x, m01):
    hi, mid, lo = _split3(x)
    return (_dot(lo, m01) + _dot(mid, m01)) + _dot(hi, m01)


def _softplus(x):
    return jnp.maximum(x, 0.0) + jnp.log1p(jnp.exp(-jnp.abs(x)))


def _silu(x):
    return x * (1.0 / (1.0 + jnp.exp(-x)))


def _sigmoid(x):
    return 1.0 / (1.0 + jnp.exp(-x))


def _ln_rows(x, g, b):
    mu = jnp.mean(x, -1, keepdims=True)
    xc = x - mu
    var = jnp.mean(xc * xc, -1, keepdims=True)
    return xc * lax.rsqrt(var + LN_EPS) * g + b


def _ln_in_kernel(xp_ref, xs_ref, g_ref, b_ref, hf_ref, hb_ref, *, n_big, m_small):
    i = pl.program_id(0)

    @pl.when(i < n_big)
    def _():
        y = _ln_rows(xp_ref[...], g_ref[...], b_ref[...])
        hf_ref[...] = y
        hb_ref[...] = y.astype(BF16)

    @pl.when(i == n_big)
    def _():
        y = _ln_rows(xs_ref[...], g_ref[...], b_ref[...])
        hf_ref[0:m_small, :] = y
        hb_ref[0:m_small, :] = y.astype(BF16)


def _ln_in(xp, xs, g, b, tr):
    mb, d = xp.shape
    ms = xs.shape[0]
    assert mb % tr == 0 and ms <= tr
    nb = mb // tr
    m = mb + ms
    return pl.pallas_call(
        functools.partial(_ln_in_kernel, n_big=nb, m_small=ms),
        grid=(nb + 1,),
        in_specs=[pl.BlockSpec((tr, d), lambda i: (jnp.minimum(i, nb - 1), 0)),
                  pl.BlockSpec((ms, d), lambda i: (0, 0)),
                  pl.BlockSpec((1, d), lambda i: (0, 0)),
                  pl.BlockSpec((1, d), lambda i: (0, 0))],
        out_specs=[pl.BlockSpec((tr, d), lambda i: (i, 0)),
                   pl.BlockSpec((tr, d), lambda i: (i, 0))],
        out_shape=[jax.ShapeDtypeStruct((m, d), F32), jax.ShapeDtypeStruct((m, d), BF16)],
        compiler_params=_cp("arbitrary"),
        name="ln_in",
    )(xp, xs, g, b)


def _mm_kernel(x_ref, w_ref, o_ref):
    o_ref[...] = _dot(x_ref[...], w_ref[...].astype(BF16)).astype(o_ref.dtype)


def _matmul(x, w, tm, tn, out_dtype=F32, name="matmul"):
    m, k = x.shape
    n = w.shape[1]
    assert m % tm == 0 and n % tn == 0
    return pl.pallas_call(
        _mm_kernel,
        grid=(m // tm, n // tn),
        in_specs=[pl.BlockSpec((tm, k), lambda i, j: (i, 0)),
                  pl.BlockSpec((k, tn), lambda i, j: (0, j))],
        out_specs=pl.BlockSpec((tm, tn), lambda i, j: (i, j)),
        out_shape=jax.ShapeDtypeStruct((m, n), out_dtype),
        compiler_params=_cp("parallel", "arbitrary"),
        name=name,
    )(x, w)


def _in_proj_kernel(a_ref, sh_ref, x_ref, *rest, n_src):
    del a_ref
    w_refs = rest[:n_src]
    o_ref, w_s = rest[n_src], rest[n_src + 1]
    j = pl.program_id(0)
    tn = o_ref.shape[1]

    @pl.when(pl.program_id(1) == 0)
    def _():
        sh = sh_ref[j]
        wide = jnp.concatenate([r[...].astype(BF16) for r in w_refs], axis=1)
        lane = lax.broadcasted_iota(jnp.int32, (1, n_src * LANES), 1)
        wide = jnp.where((lane >= sh) & (lane < sh + tn), wide, jnp.zeros_like(wide))
        ri = lax.broadcasted_iota(jnp.int32, (n_src * LANES, tn), 0)
        ci = lax.broadcasted_iota(jnp.int32, (n_src * LANES, tn), 1)
        sel = jnp.where(ri == ci + sh, 1.0, 0.0).astype(BF16)
        w_s[...] = _dot(wide, sel).astype(BF16)

    o_ref[...] = _dot(x_ref[...], w_s[...])


def _in_proj(x, w, segments, tm, tn):
    m, k = x.shape
    nin = w.shape[1]
    n_src = tn // LANES + 1
    starts = []
    for src, width in segments:
        assert width % tn == 0
        starts += [src + c for c in range(0, width, tn)]
    last = (nin - 1) // LANES
    assert all(s // LANES + n_src - 1 <= last for s in starts)
    a_tab = jnp.asarray([s // LANES for s in starts], jnp.int32)
    sh_tab = jnp.asarray([s % LANES for s in starts], jnp.int32)
    nj = len(starts)
    w_specs = [pl.BlockSpec((k, LANES), functools.partial(lambda j, i, a, sh, t: (0, a[j] + t), t=t))
               for t in range(n_src)]
    grid_spec = pltpu.PrefetchScalarGridSpec(
        num_scalar_prefetch=2,
        grid=(nj, m // tm),
        in_specs=[pl.BlockSpec((tm, k), lambda j, i, a, sh: (i, 0))] + w_specs,
        out_specs=pl.BlockSpec((tm, tn), lambda j, i, a, sh: (i, j)),
        scratch_shapes=[pltpu.VMEM((k, tn), BF16)],
    )
    return pl.pallas_call(
        functools.partial(_in_proj_kernel, n_src=n_src),
        grid_spec=grid_spec,
        out_shape=jax.ShapeDtypeStruct((m, nj * tn), F32),
        compiler_params=_cp("arbitrary", "arbitrary"),
        name="in_proj",
    )(a_tab, sh_tab, x, *([w] * n_src))


def _small_proj_kernel(x_ref, w_ref, b_ref, o_ref, *, n_f):
    a = _dot(x_ref[...], w_ref[...]) + b_ref[...]
    lane = lax.broadcasted_iota(jnp.int32, a.shape, 1)
    sp_pos = _softplus(a)
    ls = -_softplus(-a)
    o_ref[...] = jnp.where(lane < n_f, ls, sp_pos)


def _small_proj(x, w, b, tm, n_f):
    m, k = x.shape
    return pl.pallas_call(
        functools.partial(_small_proj_kernel, n_f=n_f),
        grid=(m // tm,),
        in_specs=[pl.BlockSpec((tm, k), lambda i: (i, 0)),
                  pl.BlockSpec((k, LANES), lambda i: (0, 0)),
                  pl.BlockSpec((1, LANES), lambda i: (0, 0))],
        out_specs=pl.BlockSpec((tm, LANES), lambda i: (i, 0)),
        out_shape=jax.ShapeDtypeStruct((m, LANES), F32),
        compiler_params=_cp("arbitrary"),
        name="small_proj",
    )(x, w, b)


def _cumsum_kernel(x_ref, o_ref, *, nblk):
    h = x_ref.shape[0]
    r = lax.broadcasted_iota(jnp.int32, (LANES, LANES), 0)
    c = lax.broadcasted_iota(jnp.int32, (LANES, LANES), 1)
    tri = jnp.where(r <= c, 1.0, 0.0).astype(BF16)
    carry = jnp.zeros((h, 1), F32)
    for j in range(nblk):
        blk = _sum01_right(x_ref[:, j * LANES:(j + 1) * LANES], tri) + carry
        o_ref[:, j * LANES:(j + 1) * LANES] = blk
        carry = blk[:, LANES - 1:LANES]


def _cumsum_lanes(x):
    n, h, length = x.shape
    return pl.pallas_call(
        functools.partial(_cumsum_kernel, nblk=length // LANES),
        grid=(n,),
        in_specs=[pl.BlockSpec((None, h, length), lambda i: (i, 0, 0))],
        out_specs=pl.BlockSpec((None, h, length), lambda i: (i, 0, 0)),
        out_shape=jax.ShapeDtypeStruct((n, h, length), F32),
        compiler_params=_cp("arbitrary"),
        name="logf_cumsum",
    )(x)


def _bias_lanes(f, key_side):
    hi, mid, lo = _split3(-f if key_side else f)
    lane = lax.broadcasted_iota(jnp.int32, (f.shape[0], LANES), 1)
    f0, o0 = (3, 0) if key_side else (0, 3)
    v = jnp.where(lane == f0, hi.astype(F32), jnp.where(lane == f0 + 1, mid.astype(F32), lo.astype(F32)))
    v = jnp.where((lane >= f0) & (lane < f0 + 3), v, jnp.where((lane >= o0) & (lane < o0 + 3), 1.0, 0.0))
    return v.astype(BF16)


def _fox_prompt_kernel(qr_ref, kr_ref, vr_ref, qm_ref, km_ref, vm_ref, fc_ref,
                       or_ref, om_ref, k_s, v_s, qa_s, m_s, l_s, acc_s, *, nm, bq, scale):
    h = pl.program_id(1)
    seq, dh = qr_ref.shape
    pad = LANES - nm
    nq = seq // bq
    lpx = LANES + seq

    hsel = lax.broadcasted_iota(jnp.int32, (1, fc_ref.shape[1]), 1) == h

    def fcol(start, size):
        blk = fc_ref[pl.ds(start, size), :]
        return jnp.sum(jnp.where(hsel, blk, 0.0), axis=1, keepdims=True) * LOG2E

    k_s[0:pad, 0:dh] = jnp.zeros((pad, dh), BF16)
    v_s[0:pad, :] = jnp.zeros((pad, dh), BF16)
    k_s[pad:LANES, 0:dh] = km_ref[...].astype(BF16)
    v_s[pad:LANES, :] = vm_ref[...].astype(BF16)
    k_s[0:LANES, dh:] = _bias_lanes(fcol(0, LANES), True)
    cb = 512 if seq % 512 == 0 else LANES
    for r0 in range(0, seq, cb):
        k_s[LANES + r0:LANES + r0 + cb, 0:dh] = kr_ref[r0:r0 + cb, :].astype(BF16)
        k_s[LANES + r0:LANES + r0 + cb, dh:] = _bias_lanes(fcol(LANES + r0, cb), True)
        v_s[LANES + r0:LANES + r0 + cb, :] = vr_ref[r0:r0 + cb, :].astype(BF16)

    qm = jnp.concatenate([(qm_ref[...] * scale).astype(BF16), _bias_lanes(fcol(pad, nm), False)], axis=1)
    rm = lax.broadcasted_iota(jnp.int32, (nm, LANES), 0)
    cm = lax.broadcasted_iota(jnp.int32, (nm, LANES), 1)
    sm = jnp.where((cm >= pad) & (cm - pad <= rm), _dot_nt(qm, k_s[0:LANES, :]), NEG)
    pm = jnp.exp2(sm - jnp.max(sm, axis=1, keepdims=True))
    om = _dot(pm.astype(BF16), v_s[0:LANES, :]) / jnp.sum(pm, axis=1, keepdims=True)
    om_ref[...] = om.astype(om_ref.dtype)

    rs = min(bq, 256)

    def update(r0, start, size, mask):
        s = _dot_nt(qa_s[r0:r0 + rs, :], k_s[pl.ds(start, size), :])
        if mask is not None:
            s = jnp.where(mask, s, NEG)
        m_old = m_s[r0:r0 + rs, :]
        m_new = jnp.maximum(m_old, jnp.max(s, axis=1, keepdims=True))
        alpha = jnp.exp2(m_old - m_new)
        nl = size // LANES
        p = jnp.exp2(s - (m_new if nl == 1 else jnp.concatenate([m_new] * nl, axis=1)))
        ps = p[:, 0:LANES]
        for u in range(1, nl):
            ps = ps + p[:, u * LANES:(u + 1) * LANES]
        l_s[r0:r0 + rs, :] = alpha * l_s[r0:r0 + rs, :] + ps
        acc_s[r0:r0 + rs, :] = alpha * acc_s[r0:r0 + rs, :] + _dot(p.astype(BF16), v_s[pl.ds(start, size), :])
        m_s[r0:r0 + rs, :] = m_new

    meta_cols = lax.broadcasted_iota(jnp.int32, (rs, LANES), 1) >= pad

    def qblock(i, _):
        q0 = pl.multiple_of(i * bq, bq)
        qa_s[:, 0:dh] = (qr_ref[pl.ds(q0, bq), :] * scale).astype(BF16)
        qa_s[:, dh:] = _bias_lanes(fcol(LANES + q0, bq), False)
        m_s[...] = jnp.full(m_s.shape, NEG, F32)
        l_s[...] = jnp.zeros(l_s.shape, F32)
        acc_s[...] = jnp.zeros(acc_s.shape, F32)
        for r0 in range(0, bq, rs):
            update(r0, 0, LANES, meta_cols)

        def kblock(j, c):
            for r0 in range(0, bq, rs):
                update(r0, pl.multiple_of(LANES + j * bq, LANES), bq, None)
            return c

        lax.fori_loop(0, i, kblock, 0)
        d0 = pl.multiple_of(LANES + q0, LANES)
        for r0 in range(0, bq, rs):
            size = r0 + rs
            rr = lax.broadcasted_iota(jnp.int32, (rs, size), 0)
            cc = lax.broadcasted_iota(jnp.int32, (rs, size), 1)
            update(r0, d0, size, cc <= rr + r0)
        l = jnp.sum(l_s[...], axis=1, keepdims=True)
        or_ref[pl.ds(q0, bq), :] = (acc_s[...] / l).astype(or_ref.dtype)
        return 0

    lax.fori_loop(0, nq, qblock, 0)


def _fox_prompt(p_main, f_c, *, n_b, seq, nm, n_h, dh, q_off, k_off, v_off, aw):
    bq = next(b for b in (1024, 512, 256, LANES) if seq % b == 0)
    assert dh == LANES
    qb, kb, vb = q_off // dh, k_off // dh, v_off // dh
    mrow = (n_b * seq) // nm
    lpx = LANES + seq
    real = lambda cb: pl.BlockSpec((seq, dh), lambda n, h: (n, cb + h))
    meta = lambda cb: pl.BlockSpec((nm, dh), lambda n, h: (mrow + n, cb + h))
    return pl.pallas_call(
        functools.partial(_fox_prompt_kernel, nm=nm, bq=bq, scale=dh ** -0.5 * LOG2E),
        grid=(n_b, n_h),
        in_specs=[real(qb), real(kb), real(vb), meta(qb), meta(kb), meta(vb),
                  pl.BlockSpec((None, lpx, n_h), lambda n, h: (n, 0, 0))],
        out_specs=[pl.BlockSpec((seq, dh), lambda n, h: (n, h)),
                   pl.BlockSpec((nm, dh), lambda n, h: (n, h))],
        out_shape=[jax.ShapeDtypeStruct((p_main.shape[0], aw), BF16),
                   jax.ShapeDtypeStruct((n_b * nm, aw), BF16)],
        scratch_shapes=[pltpu.VMEM((lpx, 2 * dh), BF16), pltpu.VMEM((lpx, dh), BF16),
                        pltpu.VMEM((bq, 2 * dh), BF16), pltpu.VMEM((bq, LANES), F32),
                        pltpu.VMEM((bq, LANES), F32), pltpu.VMEM((bq, dh), F32)],
        compiler_params=_cp("parallel", "arbitrary"),
        name="fox_prompt",
    )(p_main, p_main, p_main, p_main, p_main, p_main, f_c)


def _fox_sample_kernel(pt_ref, q_ref, kn_ref, vn_ref, lfn_ref, *rest, n_h, pps):
    k_refs = rest[0:pps]
    v_refs = rest[pps:2 * pps]
    lf_refs = rest[2 * pps:3 * pps]
    o_ref = rest[3 * pps]
    q_s, mb_s, m_s, l_s, acc_s, car_s, kn_s, vn_s, m12_s = rest[3 * pps + 1:]
    g = pl.program_id(1)
    ng = pl.num_programs(1)
    nq, dh = q_ref.shape
    nph = lf_refs[0].shape[0]

    def suffix_rows(lf):
        r = lf.shape[0]
        hi, mid, lo = _split3(lf)
        res = _dot(jnp.concatenate([lo, mid, hi], axis=0), m12_s[...])
        res = (res[0:r] + res[r:2 * r]) + res[2 * r:3 * r]
        return res[:, 0:LANES], res[:, LANES:2 * LANES]

    def update(s, vb):
        m_old = m_s[...]
        m_new = jnp.maximum(m_old, jnp.max(s, axis=1, keepdims=True))
        alpha = jnp.exp2(m_old - m_new)
        p = jnp.exp2(s - m_new)
        l_s[...] = alpha * l_s[...] + jnp.sum(p, axis=1, keepdims=True)
        acc_s[...] = alpha * acc_s[...] + _dot(p.astype(BF16), vb)
        m_s[...] = m_new

    @pl.when(g == 0)
    def _():
        ci = lax.broadcasted_iota(jnp.int32, (LANES, LANES), 0)
        cj = lax.broadcasted_iota(jnp.int32, (LANES, LANES), 1)
        same_h = (ci % n_h) == (cj % n_h)
        m12_s[...] = jnp.concatenate([jnp.where(same_h & (ci // n_h > cj // n_h), 1.0, 0.0),
                                      jnp.where(same_h, 1.0, 0.0)], axis=1).astype(BF16)
        q_s[...] = (q_ref[...] * (dh ** -0.5 * LOG2E)).astype(BF16)
        kn_s[...] = jnp.zeros(kn_s.shape, BF16)
        vn_s[...] = jnp.zeros(vn_s.shape, BF16)
        kn_s[0:nq, :] = kn_ref[...].astype(BF16)
        vn_s[0:nq, :] = vn_ref[...].astype(BF16)
        within, total = suffix_rows(lfn_ref[...])
        sfx = within[0:1, :] * LOG2E
        ri = lax.broadcasted_iota(jnp.int32, (nq, LANES), 0)
        li = lax.broadcasted_iota(jnp.int32, (nq, LANES), 1)
        rowc = -jnp.sum(jnp.where(ri == li, sfx, 0.0), axis=1, keepdims=True)
        head_ok = (ri % n_h) == (li % n_h)
        mb_s[...] = jnp.where(head_ok, rowc, NEG)
        m_s[...] = jnp.full(m_s.shape, NEG, F32)
        l_s[...] = jnp.zeros(l_s.shape, F32)
        acc_s[...] = jnp.zeros(acc_s.shape, F32)
        s = _dot_nt(q_s[...], kn_s[...]) + sfx + mb_s[...]
        s = jnp.where(li // n_h <= ri // n_h, s, NEG)
        update(s, vn_s[...])
        car_s[...] = total[0:1, :]

    q = q_s[...]
    mb = mb_s[...]
    prow = k_refs[0].shape[0]
    rows = lax.broadcasted_iota(jnp.int32, (nph, LANES), 0)
    within_all, total_all = suffix_rows(jnp.concatenate([lf_refs[j][...] for j in range(pps)], axis=0))
    car = car_s[...]
    s_parts = []
    for j in range(pps):
        total = total_all[j * nph:(j + 1) * nph]
        later = jnp.zeros((nph, LANES), F32)
        for r in range(1, nph):
            later = later + jnp.where(rows < r, total[r:r + 1, :], 0.0)
        bias = (within_all[j * nph:(j + 1) * nph] + later + car) * LOG2E
        car = car + jnp.sum(total, axis=0, keepdims=True)
        s = _dot_nt(q, k_refs[j][...].astype(BF16))
        s_parts += [s[:, r * LANES:(r + 1) * LANES] + (bias[r:r + 1, :] + mb) for r in range(nph)]
    car_s[...] = car
    s = jnp.concatenate(s_parts, axis=1)
    m_old = m_s[...]
    m_new = jnp.maximum(m_old, jnp.max(s, axis=1, keepdims=True))
    alpha = jnp.exp2(m_old - m_new)
    p = jnp.exp2(s - m_new)
    l_s[...] = alpha * l_s[...] + jnp.sum(p, axis=1, keepdims=True)
    pb = p.astype(BF16)
    pv = _dot(pb[:, 0:prow], v_refs[0][...].astype(BF16))
    for j in range(1, pps):
        pv = pv + _dot(pb[:, j * prow:(j + 1) * prow], v_refs[j][...].astype(BF16))
    acc_s[...] = alpha * acc_s[...] + pv
    m_s[...] = m_new

    @pl.when(g == ng - 1)
    def _():
        o_ref[...] = acc_s[...] / l_s[...]


def _fox_sample(q, kn, vn, lfn, ck, cv, clf, page_table, *, n_h, pps):
    nb, nq, dh = q.shape
    npg = page_table.shape[1]
    assert npg % pps == 0 and nq <= LANES and LANES % n_h == 0
    prow = ck.shape[1]
    nph = clf.shape[1]
    seq_spec = lambda r: pl.BlockSpec((None, r, dh), lambda b, g, pt: (b, 0, 0))

    def page_spec(rows, width, j):
        return pl.BlockSpec((None, rows, width), lambda b, g, pt: (pt[b, npg - 1 - (g * pps + j)], 0, 0))

    in_specs = [seq_spec(nq), seq_spec(nq), seq_spec(nq), pl.BlockSpec((None, 16, LANES), lambda b, g, pt: (b, 0, 0))]
    in_specs += [page_spec(prow, dh, j) for j in range(pps)]
    in_specs += [page_spec(prow, dh, j) for j in range(pps)]
    in_specs += [page_spec(nph, LANES, j) for j in range(pps)]
    grid_spec = pltpu.PrefetchScalarGridSpec(
        num_scalar_prefetch=1,
        grid=(nb, npg // pps),
        in_specs=in_specs,
        out_specs=pl.BlockSpec((None, nq, dh), lambda b, g, pt: (b, 0, 0)),
        scratch_shapes=[pltpu.VMEM((nq, dh), BF16), pltpu.VMEM((nq, LANES), F32),
                        pltpu.VMEM((nq, 1), F32), pltpu.VMEM((nq, 1), F32), pltpu.VMEM((nq, dh), F32),
                        pltpu.VMEM((1, LANES), F32),
                        pltpu.VMEM((LANES, dh), BF16), pltpu.VMEM((LANES, dh), BF16),
                        pltpu.VMEM((LANES, 2 * LANES), BF16)],
    )
    return pl.pallas_call(
        functools.partial(_fox_sample_kernel, n_h=n_h, pps=pps),
        grid_spec=grid_spec,
        out_shape=jax.ShapeDtypeStruct((nb, nq, dh), F32),
        compiler_params=_cp("parallel", "arbitrary"),
        name="fox_sample",
    )(page_table, q, kn, vn, lfn, *([ck] * pps), *([cv] * pps), *([clf] * pps))


CH = 128


def _ssd_kernel(xs_ref, b_ref, c_ref, dtc_ref, dtr_ref, alc_ref, alr_ref, dsk_ref,
                wx_ref, wb_ref, wc_ref, bx_ref, bb_ref, bc_ref, px_ref, pb_ref, pc_ref, h0_ref,
                y_ref, hT_ref, ex_s, eb_s, ec_s, h_s, *, t_valid, n_sub, n_e, p_dim):
    c = pl.program_id(2)
    nc = pl.num_programs(2)
    kw = wx_ref.shape[0]
    base = 8
    rows = CH * n_sub
    rows_in = t_valid if t_valid < CH else rows

    @pl.when(c == 0)
    def _():
        for e_s, p_ref in ((ex_s, px_ref), (eb_s, pb_ref), (ec_s, pc_ref)):
            e_s[...] = jnp.zeros(e_s.shape, F32)
            e_s[base - (kw - 1):base, :] = p_ref[...]
        h_s[...] = h0_ref[...].reshape(h_s.shape)

    lq = CH if t_valid == CH else 16 * ((t_valid + 15) // 16)
    rows_c = rows if t_valid == CH else lq

    def conv_silu(e_s, u_ref, w_ref, bias_ref):
        e_s[base:base + rows_in, :] = u_ref[...]
        out = bias_ref[...]
        for j in range(kw):
            out = out + e_s[base - (kw - 1) + j:base - (kw - 1) + j + rows_c, :] * w_ref[j:j + 1, :]
        return _silu(out)

    xc_all = conv_silu(ex_s, xs_ref, wx_ref, bx_ref)
    bm_all = conv_silu(eb_s, b_ref, wb_ref, bb_ref)
    cm_all = conv_silu(ec_s, c_ref, wc_ref, bc_ref)
    if t_valid == CH:
        for e_s in (ex_s, eb_s, ec_s):
            e_s[0:base, :] = e_s[rows:rows + base, :]
    else:
        valid = lax.broadcasted_iota(jnp.int32, (lq, 1), 0) < t_valid
        xc_all = jnp.concatenate([jnp.where(valid, xc_all, 0.0), jnp.zeros((CH - lq, xc_all.shape[1]), F32)], axis=0)
        bm_all = jnp.concatenate([jnp.where(valid, bm_all, 0.0), jnp.zeros((CH - lq, bm_all.shape[1]), F32)], axis=0)

    neg_a_c = -jnp.exp(alc_ref[...])
    neg_a_r = -jnp.exp(alr_ref[...])
    ri = lax.broadcasted_iota(jnp.int32, (CH, CH), 0)
    li = lax.broadcasted_iota(jnp.int32, (CH, CH), 1)
    causal = li <= ri
    tril = jnp.where(causal, 1.0, 0.0).astype(BF16)
    triu = jnp.where(ri <= li, 1.0, 0.0).astype(BF16)
    lane2 = lax.broadcasted_iota(jnp.int32, (CH, 2 * p_dim), 1) < p_dim
    row2 = lax.broadcasted_iota(jnp.int32, (2 * p_dim, 1), 0) < p_dim
    causal_l = lax.broadcasted_iota(jnp.int32, (lq, CH), 1) <= lax.broadcasted_iota(jnp.int32, (lq, CH), 0)
    lane2_l = lax.broadcasted_iota(jnp.int32, (lq, 2 * p_dim), 1) < p_dim

    for sub in range(n_sub):
        r0 = sub * CH
        xc = xc_all[r0:r0 + CH]
        dtc = dtc_ref[r0:r0 + CH, :]
        dtr = dtr_ref[:, r0:r0 + CH]
        cum_c = _sum01_left(tril, dtc * neg_a_c) * LOG2E
        cum_r = _sum01_right(dtr * neg_a_r, triu) * LOG2E
        end_c = cum_c[CH - 1:CH, :]
        dd_c = jnp.exp2(end_c - cum_c) * dtc
        ecum_c = jnp.exp2(cum_c)
        cdec_r = jnp.exp2(cum_r[:, CH - 1:CH])

        xb = xc.astype(BF16)
        bmb = bm_all[r0:r0 + CH].astype(BF16)
        cmb = cm_all[r0:r0 + lq].astype(BF16)
        cb = _dot_nt(cmb, bmb)
        cum_l, ecum_l = cum_c[0:lq], ecum_c[0:lq]
        ys = []
        for pr in range(n_e // 2):
            e0, e1 = 2 * pr, 2 * pr + 1
            xp = xb[:, e0 * p_dim:(e1 + 1) * p_dim]
            xpf = xc[:, e0 * p_dim:(e1 + 1) * p_dim]
            yd = []
            for e in (e0, e1):
                seg = cum_l[:, e:e + 1] - cum_r[e:e + 1, :]
                dec = jnp.exp2(jnp.where(causal_l, seg, -jnp.inf))
                w = (cb * dec * dtr[e:e + 1, :]).astype(BF16)
                yd.append(_dot(w, xp))
            y_diag = jnp.where(lane2_l, yd[0], yd[1])
            hp = h_s[e0 * p_dim:(e1 + 1) * p_dim, :]
            y_off = _dot_nt(cmb, hp.astype(BF16)) * jnp.where(lane2_l, ecum_l[:, e0:e0 + 1], ecum_l[:, e1:e1 + 1])
            ys.append(y_diag + y_off + xpf[0:lq] * dsk_ref[:, e0 * p_dim:(e1 + 1) * p_dim])
            xw = (xpf * jnp.where(lane2, dd_c[:, e0:e0 + 1], dd_c[:, e1:e1 + 1])).astype(BF16)
            st = _dot_tn(xw, bmb)
            cd = jnp.where(row2, cdec_r[e0:e0 + 1, :], cdec_r[e1:e1 + 1, :])
            h_s[e0 * p_dim:(e1 + 1) * p_dim, :] = hp * cd + st
        y = jnp.concatenate(ys, axis=1)
        if t_valid < CH:
            y_ref[...] = y[0:t_valid, :]
        else:
            y_ref[r0:r0 + CH, :] = y

    @pl.when(c == nc - 1)
    def _():
        hT_ref[...] = h_s[...].reshape(hT_ref.shape)


def _ssd(xsrc, col_x, col_b, col_c, t_valid, n_seq, n_chunks, dtc, dtr, alc, alr, dsk,
         conv_w, conv_b, prev, h0, *, n_g, n_e, p_dim, n_state, d_inner, y_rows=None):
    gw = n_e * p_dim
    kw = conv_w.shape[0]
    three_d = xsrc.ndim == 3
    assert three_d or t_valid == CH
    n_sub = 2 if (not three_d and n_chunks % 2 == 0) else 1
    n_steps = n_chunks // n_sub
    rows = CH * n_sub
    row_of = lambda n, c: n * n_steps + c

    def src_spec(width, colblk):
        if three_d:
            return pl.BlockSpec((None, t_valid, width), lambda n, g, c: (n, 0, colblk(g)))
        return pl.BlockSpec((rows, width), lambda n, g, c: (row_of(n, c), colblk(g)))

    cx = lambda g: col_x // gw + g
    cbk = lambda g: col_b // n_state + g
    cck = lambda g: col_c // n_state + g
    wx = lambda g: g
    wb = lambda g: d_inner // n_state + g
    wc = lambda g: (d_inner + n_g * n_state) // n_state + g

    if three_d:
        dtc_spec = pl.BlockSpec((None, None, CH, n_e), lambda n, g, c: (g, n, 0, 0))
        dtr_spec = pl.BlockSpec((None, n_e, CH), lambda n, g, c: (n, g, 0))
        y_spec = pl.BlockSpec((None, t_valid, gw), lambda n, g, c: (n, 0, g))
        y_shape = jax.ShapeDtypeStruct((n_seq, t_valid, d_inner), F32)
    else:
        dtc_spec = pl.BlockSpec((None, rows, n_e), lambda n, g, c: (g, row_of(n, c), 0))
        dtr_spec = pl.BlockSpec((n_e, rows), lambda n, g, c: (g, row_of(n, c)))
        y_spec = pl.BlockSpec((rows, gw), lambda n, g, c: (row_of(n, c), g))
        y_shape = jax.ShapeDtypeStruct((y_rows or n_seq * n_chunks * CH, d_inner), F32)

    in_specs = [
        src_spec(gw, cx), src_spec(n_state, cbk), src_spec(n_state, cck),
        dtc_spec, dtr_spec,
        pl.BlockSpec((None, 1, n_e), lambda n, g, c: (g, 0, 0)),
        pl.BlockSpec((n_e, 1), lambda n, g, c: (g, 0)),
        pl.BlockSpec((1, gw), lambda n, g, c: (0, g)),
        pl.BlockSpec((kw, gw), lambda n, g, c: (0, wx(g))),
        pl.BlockSpec((kw, n_state), lambda n, g, c: (0, wb(g))),
        pl.BlockSpec((kw, n_state), lambda n, g, c: (0, wc(g))),
        pl.BlockSpec((1, gw), lambda n, g, c: (0, wx(g))),
        pl.BlockSpec((1, n_state), lambda n, g, c: (0, wb(g))),
        pl.BlockSpec((1, n_state), lambda n, g, c: (0, wc(g))),
        pl.BlockSpec((None, kw - 1, gw), lambda n, g, c: (n, 0, wx(g))),
        pl.BlockSpec((None, kw - 1, n_state), lambda n, g, c: (n, 0, wb(g))),
        pl.BlockSpec((None, kw - 1, n_state), lambda n, g, c: (n, 0, wc(g))),
        pl.BlockSpec((None, n_e, p_dim, n_state), lambda n, g, c: (n, g, 0, 0)),
    ]
    out_specs = [y_spec, pl.BlockSpec((None, n_e, p_dim, n_state), lambda n, g, c: (n, g, 0, 0))]
    out_shape = [y_shape, jax.ShapeDtypeStruct((n_seq, n_g * n_e, p_dim, n_state), F32)]
    return pl.pallas_call(
        functools.partial(_ssd_kernel, t_valid=t_valid, n_sub=n_sub, n_e=n_e, p_dim=p_dim),
        grid=(n_seq, n_g, n_steps),
        in_specs=in_specs, out_specs=out_specs, out_shape=out_shape,
        scratch_shapes=[pltpu.VMEM((rows + 8, gw), F32), pltpu.VMEM((rows + 8, n_state), F32),
                        pltpu.VMEM((rows + 8, n_state), F32), pltpu.VMEM((n_e * p_dim, n_state), F32)],
        compiler_params=_cp("parallel", "parallel", "arbitrary"),
        name="ssd",
    )(xsrc, xsrc, xsrc, dtc, dtr, alc, alr, dsk, conv_w, conv_w, conv_w, conv_b, conv_b, conv_b,
      prev, prev, prev, h0)


def _gated_norm_kernel(y_ref, z_ref, w_ref, o_ref):
    g = y_ref[...] * _silu(z_ref[...])
    ms = jnp.mean(g * g, -1, keepdims=True)
    o_ref[...] = (g * lax.rsqrt(ms + RMS_EPS) * w_ref[...]).astype(o_ref.dtype)


def _gated_norm(y, p_main, z_off, w, tm):
    m, di = y.shape
    assert z_off % di == 0
    zb = z_off // di
    return pl.pallas_call(
        _gated_norm_kernel,
        grid=(m // tm,),
        in_specs=[pl.BlockSpec((tm, di), lambda i: (i, 0)),
                  pl.BlockSpec((tm, di), lambda i: (i, zb)),
                  pl.BlockSpec((1, di), lambda i: (0, 0))],
        out_specs=pl.BlockSpec((tm, di), lambda i: (i, 0)),
        out_shape=jax.ShapeDtypeStruct((m, di), BF16),
        compiler_params=_cp("arbitrary"),
        name="gated_rmsnorm",
    )(y, p_main, w)


def _proj_gate_kernel(x_ref, w_ref, g_ref, *rest, add):
    if add:
        a_ref, o_ref = rest
    else:
        (o_ref,) = rest
    v = _sigmoid(g_ref[...]) * _dot(x_ref[...], w_ref[...])
    if add:
        v = v + a_ref[...]
    o_ref[...] = v.astype(o_ref.dtype)


def _proj_gate(x, w, p_main, g_off, addend, tm, tn, out_dtype):
    m, k = x.shape
    n = w.shape[1]
    assert g_off % tn == 0 and n % tn == 0
    gb = g_off // tn
    in_specs = [pl.BlockSpec((tm, k), lambda i, j: (i, 0)),
                pl.BlockSpec((k, tn), lambda i, j: (0, j)),
                pl.BlockSpec((tm, tn), lambda i, j: (i, gb + j))]
    args = [x, w, p_main]
    if addend is not None:
        in_specs.append(pl.BlockSpec((tm, tn), lambda i, j: (i, j)))
        args.append(addend)
    return pl.pallas_call(
        functools.partial(_proj_gate_kernel, add=addend is not None),
        grid=(m // tm, n // tn),
        in_specs=in_specs,
        out_specs=pl.BlockSpec((tm, tn), lambda i, j: (i, j)),
        out_shape=jax.ShapeDtypeStruct((m, n), out_dtype),
        compiler_params=_cp("parallel", "arbitrary"),
        name="proj_gate",
    )(*args)


def _mm_res_ln_kernel(x_ref, w_ref, r_ref, g_ref, b_ref, of_ref, ob_ref, acc_s, *, alpha):
    k = pl.program_id(1)

    @pl.when(k == 0)
    def _():
        acc_s[...] = alpha * r_ref[...]

    acc_s[...] += _dot(x_ref[...], w_ref[...].astype(BF16))

    @pl.when(k == pl.num_programs(1) - 1)
    def _():
        y = _ln_rows(acc_s[...], g_ref[...], b_ref[...])
        of_ref[...] = y
        ob_ref[...] = y.astype(BF16)


def _mm_res_ln(x, w, res, g, b, alpha, tm, tk):
    m, k = x.shape
    n = w.shape[1]
    assert m % tm == 0 and k % tk == 0
    return pl.pallas_call(
        functools.partial(_mm_res_ln_kernel, alpha=alpha),
        grid=(m // tm, k // tk),
        in_specs=[pl.BlockSpec((tm, tk), lambda i, kk: (i, kk)),
                  pl.BlockSpec((tk, n), lambda i, kk: (kk, 0)),
                  pl.BlockSpec((tm, n), lambda i, kk: (i, 0)),
                  pl.BlockSpec((1, n), lambda i, kk: (0, 0)),
                  pl.BlockSpec((1, n), lambda i, kk: (0, 0))],
        out_specs=[pl.BlockSpec((tm, n), lambda i, kk: (i, 0)), pl.BlockSpec((tm, n), lambda i, kk: (i, 0))],
        out_shape=[jax.ShapeDtypeStruct((m, n), F32), jax.ShapeDtypeStruct((m, n), BF16)],
        scratch_shapes=[pltpu.VMEM((tm, n), F32)],
        compiler_params=_cp("parallel", "arbitrary"),
        name="mm_res_ln",
    )(x, w, res, g, b)


def _put_rows_kernel(dst_ref, src_ref, o_ref):
    del dst_ref
    o_ref[...] = src_ref[...]


def _put_rows(dst, src, row0, rb):
    n, w = src.shape
    assert row0 % rb == 0 and n % rb == 0 and row0 + n <= dst.shape[0]
    return pl.pallas_call(
        _put_rows_kernel,
        grid=(n // rb,),
        in_specs=[pl.BlockSpec(memory_space=pl.ANY), pl.BlockSpec((rb, w), lambda i: (i, 0))],
        out_specs=pl.BlockSpec((rb, w), lambda i: (row0 // rb + i, 0)),
        out_shape=jax.ShapeDtypeStruct(dst.shape, dst.dtype),
        input_output_aliases={0: 0},
        compiler_params=_cp("arbitrary"),
        name="put_rows",
    )(dst, src)


def _kv_states_kernel(kc_ref, kp_ref, vc_ref, vp_ref, ko_ref, vo_ref, *, n_h, nm):
    rows, aw = kc_ref.shape
    dh = aw // n_h
    for c_ref, p_ref, o_ref in ((kc_ref, kp_ref, ko_ref), (vc_ref, vp_ref, vo_ref)):
        for h in range(n_h):
            o_ref[pl.ds(h, nm, stride=n_h), :] = p_ref[:, h * dh:(h + 1) * dh]
            o_ref[pl.ds(nm * n_h + h, rows - nm, stride=n_h), :] = c_ref[0:rows - nm, h * dh:(h + 1) * dh]


def _kv_states(p_main, *, n_b, seq, nm, n_h, dh, k_off, v_off, rows):
    aw = n_h * dh
    assert k_off % aw == 0 and v_off % aw == 0 and seq % rows == 0 and rows % nm == 0 and (n_b * seq) % nm == 0
    nsb = seq // rows
    mb = n_b * seq
    cur = lambda cb: pl.BlockSpec((rows, aw), lambda n, i: (n * nsb + jnp.minimum(i, nsb - 1), cb))
    prev = lambda cb: pl.BlockSpec(
        (nm, aw), lambda n, i: (jnp.where(i == 0, mb // nm + n, (n * seq + rows * i) // nm - 1), cb))
    out_spec = pl.BlockSpec((None, rows * n_h, dh), lambda n, i: (n, i, 0))
    shape = jax.ShapeDtypeStruct((n_b, (nm + seq) * n_h, dh), F32)
    k, v = pl.pallas_call(
        functools.partial(_kv_states_kernel, n_h=n_h, nm=nm),
        grid=(n_b, nsb + 1),
        in_specs=[cur(k_off // aw), prev(k_off // aw), cur(v_off // aw), prev(v_off // aw)],
        out_specs=[out_spec, out_spec], out_shape=[shape, shape],
        compiler_params=_cp("arbitrary", "arbitrary"),
        name="kv_states",
    )(p_main, p_main, p_main, p_main)
    return k.reshape(n_b, nm + seq, n_h, dh), v.reshape(n_b, nm + seq, n_h, dh)


def _ffn_conv_small_kernel(*refs, kw):
    ua = refs[0:kw]
    ub = refs[kw:2 * kw]
    wa_ref, wb_ref, ba_ref, bb_ref, o_ref = refs[2 * kw:]
    a = ba_ref[...]
    b = bb_ref[...]
    for j in range(kw):
        a = a + ua[j][...] * wa_ref[j:j + 1, :]
        b = b + ub[j][...] * wb_ref[j:j + 1, :]
    o_ref[...] = (_silu(a) * b).astype(o_ref.dtype)


def _ffn_conv_small(taps, conv_w, conv_b, *, dff, tf):
    kw = conv_w.shape[0]
    rows = taps[0].shape[0]
    nj = dff // tf
    a_specs = [pl.BlockSpec((rows, tf), lambda j: (0, j)) for _ in range(kw)]
    b_specs = [pl.BlockSpec((rows, tf), lambda j: (0, nj + j)) for _ in range(kw)]
    return pl.pallas_call(
        functools.partial(_ffn_conv_small_kernel, kw=kw),
        grid=(nj,),
        in_specs=a_specs + b_specs + [pl.BlockSpec((kw, tf), lambda j: (0, j)),
                                      pl.BlockSpec((kw, tf), lambda j: (0, nj + j)),
                                      pl.BlockSpec((1, tf), lambda j: (0, j)),
                                      pl.BlockSpec((1, tf), lambda j: (0, nj + j))],
        out_specs=pl.BlockSpec((rows, tf), lambda j: (0, j)),
        out_shape=jax.ShapeDtypeStruct((rows, dff), BF16),
        compiler_params=_cp("arbitrary"),
        name="ffn_conv_small",
    )(*taps, *taps, conv_w, conv_w, conv_b, conv_b)


def _ffn_down_kernel(ua_ref, ub_ref, ha_ref, hb_ref, wa_ref, wb_ref, ba_ref, bb_ref, gs_ref, w_ref, r_ref,
                     g_ref, b_ref, o0_ref, o1_ref, acc_s, ea_s, eb_s, *, alpha, n_big, m_small, split):
    i = pl.program_id(0)
    k = pl.program_id(1)
    kw = wa_ref.shape[0]
    tm = ua_ref.shape[0]
    base = 8

    @pl.when(k == 0)
    def _():
        acc_s[...] = alpha * r_ref[...]

    def conv(e_s, u_ref, h_ref, w_ref, bias_ref):
        e_s[0:base, :] = h_ref[...]
        e_s[base:base + tm, :] = u_ref[...]
        out = bias_ref[...]
        for j in range(kw):
            out = out + e_s[base - (kw - 1) + j:base - (kw - 1) + j + tm, :] * w_ref[j:j + 1, :]
        return out

    @pl.when(i < n_big)
    def _():
        a = conv(ea_s, ua_ref, ha_ref, wa_ref, ba_ref)
        b = conv(eb_s, ub_ref, hb_ref, wb_ref, bb_ref)
        acc_s[...] += _dot((_silu(a) * b).astype(BF16), w_ref[...])

    @pl.when(i == n_big)
    def _():
        acc_s[0:m_small, :] += _dot(gs_ref[...], w_ref[...])

    @pl.when(k == pl.num_programs(1) - 1)
    def _():
        y = _ln_rows(acc_s[...], g_ref[...], b_ref[...])
        if not split:
            o0_ref[...] = y
            o1_ref[...] = y.astype(BF16)
        else:
            @pl.when(i < n_big)
            def _():
                o0_ref[...] = y

            @pl.when(i == n_big)
            def _():
                o1_ref[...] = y[0:m_small, :]


def _ffn_down(u, halo, g_small, conv_w, conv_b, w, res, g, b, alpha, *, m_big, dff, tm, tk, split):
    m = u.shape[0]
    n = w.shape[1]
    kw = conv_w.shape[0]
    nj = dff // tk
    m_small = m - m_big
    assert m_big % tm == 0 and 0 < m_small <= tm and dff % tk == 0
    n_big = m_big // tm
    big = lambda i: jnp.minimum(i, n_big - 1)
    if split:
        out_specs = [pl.BlockSpec((tm, n), lambda i, k: (big(i), 0)), pl.BlockSpec((m_small, n), lambda i, k: (0, 0))]
        out_shape = [jax.ShapeDtypeStruct((m_big, n), F32), jax.ShapeDtypeStruct((m_small, n), F32)]
    else:
        out_specs = [pl.BlockSpec((tm, n), lambda i, k: (i, 0)), pl.BlockSpec((tm, n), lambda i, k: (i, 0))]
        out_shape = [jax.ShapeDtypeStruct((m, n), F32), jax.ShapeDtypeStruct((m, n), BF16)]
    return pl.pallas_call(
        functools.partial(_ffn_down_kernel, alpha=alpha, n_big=n_big, m_small=m_small, split=split),
        grid=(n_big + 1, nj),
        in_specs=[pl.BlockSpec((tm, tk), lambda i, k: (big(i), k)),
                  pl.BlockSpec((tm, tk), lambda i, k: (big(i), nj + k)),
                  pl.BlockSpec((None, 8, tk), lambda i, k: (big(i), 0, k)),
                  pl.BlockSpec((None, 8, tk), lambda i, k: (big(i), 0, nj + k)),
                  pl.BlockSpec((kw, tk), lambda i, k: (0, k)),
                  pl.BlockSpec((kw, tk), lambda i, k: (0, nj + k)),
                  pl.BlockSpec((1, tk), lambda i, k: (0, k)),
                  pl.BlockSpec((1, tk), lambda i, k: (0, nj + k)),
                  pl.BlockSpec((m_small, tk), lambda i, k: (0, k)),
                  pl.BlockSpec((tk, n), lambda i, k: (k, 0)),
                  pl.BlockSpec((tm, n), lambda i, k: (i, 0)),
                  pl.BlockSpec((1, n), lambda i, k: (0, 0)),
                  pl.BlockSpec((1, n), lambda i, k: (0, 0))],
        out_specs=out_specs, out_shape=out_shape,
        scratch_shapes=[pltpu.VMEM((tm, n), F32), pltpu.VMEM((tm + 8, tk), F32), pltpu.VMEM((tm + 8, tk), F32)],
        compiler_params=_cp("arbitrary", "arbitrary"),
        name="ffn_down",
    )(u, u, halo, halo, conv_w, conv_w, conv_b, conv_b, g_small, w, res, g, b)


def _shift_taps(u_seq, prev):
    n, t, c = u_seq.shape
    k1 = prev.shape[1]
    full = jnp.concatenate([prev.astype(u_seq.dtype), u_seq], axis=1)
    return [full[:, j:j + t].reshape(n * t, c) for j in range(k1 + 1)]


def _layer(hf, hb, lw, st, dims, page_table, last):
    (w_in, b_f, w_att_out, ssm_conv_w, ssm_conv_b, dt_bias, a_log, d_skip, ssm_norm_w,
     w_ssm_out, w_o, ln1_g, ln1_b, w_up, ffn_conv_w, ffn_conv_b, w_down, ln2_g, ln2_b) = lw
    cache_k, cache_v, cache_logf, conv_ssm_s, ssm_s, conv_ffn_s = st
    d = dims
    B, SEQ, NM, NB, T, D = d["B"], d["SEQ"], d["NM"], d["NB"], d["T"], d["D"]
    H, DH, AW, DI, XBC, SH = d["H"], d["DH"], d["AW"], d["DI"], d["XBC"], d["SH"]
    P, N, G, E, DFF, alpha = d["P"], d["N"], d["G"], d["E"], d["DFF"], d["alpha"]
    M = hf.shape[0]
    MB, MM, MS = B * SEQ, B * NM, NB * T
    tm = _row_tile(M, 1392)
    tm_s = _row_tile(M, 512)

    o = [0]
    for s in (AW, AW, AW, H, DI, XBC, SH, D):
        o.append(o[-1] + s)
    wf = w_in[:, o[3]:o[3] + H]
    wdt = w_in[:, o[6]:o[6] + SH]
    segments = ((o[4], DI), (o[0], 3 * AW), (o[5], XBC), (o[7], 2 * D))
    z_off, q_off, k_off, v_off = 0, DI, DI + AW, DI + 2 * AW
    x_off = DI + 3 * AW
    ga_off = x_off + XBC
    gs_off = ga_off + D
    nmain = gs_off + D
    w_small = jnp.concatenate([wf, wdt, jnp.zeros((D, LANES - H - SH), F32)], axis=1).astype(BF16)
    b_small = jnp.concatenate([b_f, dt_bias, jnp.zeros((LANES - H - SH,), F32)])[None, :]

    p_main = _in_proj(hb, w_in, segments, tm, 512)
    s_small = _small_proj(hb, w_small, b_small, tm, H)
    logf = s_small[:, :H]
    dt = s_small[:, H:H + SH]

    pad = LANES - NM
    lf_real = logf[:MB].reshape(B, SEQ, H)
    lf_meta = logf[MB:MB + MM].reshape(B, NM, H)
    lf_ext = jnp.concatenate([jnp.zeros((B, pad, H), F32), lf_meta, lf_real], axis=1)
    f_c = _cumsum_lanes(lf_ext.transpose(0, 2, 1)).transpose(0, 2, 1)
    att, o_meta = _fox_prompt(p_main, f_c, n_b=B, seq=SEQ, nm=NM, n_h=H, dh=DH,
                              q_off=q_off, k_off=k_off, v_off=v_off, aw=AW)
    rb = 32
    assert MB % rb == 0 and (MM + MS) % rb == 0

    p_s = p_main[MB + MM:]
    q_s = p_s[:, q_off:q_off + AW].reshape(NB, T * H, DH)
    k_s = p_s[:, k_off:k_off + AW].reshape(NB, T * H, DH)
    v_s = p_s[:, v_off:v_off + AW].reshape(NB, T * H, DH)
    lf_s = logf[MB + MM:].reshape(NB, T * H)
    lfn = jnp.zeros((NB, 16, LANES), F32).at[:, 0, :T * H].set(lf_s)
    pool, PAGE = cache_k.shape[0], cache_k.shape[1]
    ck = cache_k.reshape(pool, PAGE * H, DH)
    cv = cache_v.reshape(pool, PAGE * H, DH)
    clf = cache_logf.astype(F32).reshape(pool, (PAGE * H) // LANES, LANES)
    npg = page_table.shape[1]
    pps = next(p for p in (8, 4, 2, 1) if npg % p == 0)
    o_s = _fox_sample(q_s, k_s, v_s, lfn, ck, cv, clf, page_table, n_h=H, pps=pps)
    o_samp = o_s.reshape(NB * T, AW).astype(BF16)
    att = _put_rows(att, jnp.concatenate([o_meta, o_samp], axis=0), MB, rb)

    dt_c = dt.reshape(M, G, E).transpose(1, 0, 2)
    dt_r = dt.T
    alc = a_log.reshape(G, 1, E)
    alr = a_log.reshape(SH, 1)
    dsk = jnp.repeat(d_skip, P)[None, :]
    cbias = ssm_conv_b[None, :]
    kw = ssm_conv_w.shape[0]
    ssd = functools.partial(_ssd, n_g=G, n_e=E, p_dim=P, n_state=N, d_inner=DI)
    col_b = x_off + DI
    col_c = col_b + G * N
    zeros_prev = jnp.zeros((B, kw - 1, XBC), F32)
    zeros_h = jnp.zeros((B, SH, P, N), F32)

    def short_dt(lo, n_seq, t):
        c3 = dt_c[:, lo:lo + n_seq * t].reshape(G, n_seq, t, E)
        r3 = dt_r[:, lo:lo + n_seq * t].reshape(SH, n_seq, t).transpose(1, 0, 2)
        return (jnp.pad(c3, ((0, 0), (0, 0), (0, CH - t), (0, 0))), jnp.pad(r3, ((0, 0), (0, 0), (0, CH - t))))

    p_m3 = p_main[MB:MB + MM].reshape(B, NM, nmain)
    dt_c_m, dt_r_m = short_dt(MB, B, NM)
    y_meta, h_meta = ssd(p_m3, x_off, col_b, col_c, NM, B, 1, dt_c_m, dt_r_m, alc, alr, dsk,
                         ssm_conv_w, cbias, zeros_prev, zeros_h)
    prev_real = p_m3[:, NM - (kw - 1):, x_off:x_off + XBC]
    y_all, h_real = ssd(p_main, x_off, col_b, col_c, CH, B, SEQ // CH, dt_c, dt_r, alc, alr, dsk,
                        ssm_conv_w, cbias, prev_real, h_meta, y_rows=M)
    p_s3 = p_s.reshape(NB, T, nmain)
    dt_c_s, dt_r_s = short_dt(MB + MM, NB, T)
    y_samp, h_samp = ssd(p_s3, x_off, col_b, col_c, T, NB, 1, dt_c_s, dt_r_s, alc, alr, dsk,
                         ssm_conv_w, cbias, conv_ssm_s, ssm_s)
    y_small = jnp.concatenate([y_meta.reshape(MM, DI), y_samp.reshape(MS, DI)], axis=0)
    y_all = _put_rows(y_all, y_small, MB, rb)
    conv_ssm_p = jnp.stack([p_main[(n + 1) * SEQ - (kw - 1):(n + 1) * SEQ, x_off:x_off + XBC] for n in range(B)],
                           axis=0)
    xbc_s = p_s3[:, :, x_off:x_off + XBC]
    conv_ssm_new_s = jnp.concatenate([conv_ssm_s, xbc_s], axis=1)[:, T:]

    yn = _gated_norm(y_all, p_main, z_off, ssm_norm_w[None, :], tm_s)

    tn_o = 512 if D % 512 == 0 else D
    tm_p = _row_tile(M, 928)
    m1 = _proj_gate(att, w_att_out.astype(BF16), p_main, ga_off, None, tm_p, tn_o, F32)
    merged = _proj_gate(yn, w_ssm_out.astype(BF16), p_main, gs_off, m1, tm_p, tn_o, BF16)
    x1f, x1b = _mm_res_ln(merged, w_o, hf, ln1_g[None, :], ln1_b[None, :], alpha, tm_s,
                          1024 if D % 1024 == 0 else 512)

    u = _matmul(x1b, w_up, tm, 512, F32, name="ffn_up")
    kf = ffn_conv_w.shape[0]
    fbias = ffn_conv_b[None, :]
    tf = 512 if DFF % 512 == 0 else DFF
    u_meta = u[MB:MB + MM].reshape(B, NM, 2 * DFF)
    u_samp = u[MB + MM:].reshape(NB, T, 2 * DFF)
    taps_m = _shift_taps(u_meta, jnp.zeros((B, kf - 1, 2 * DFF), F32))
    taps_s = _shift_taps(u_samp, conv_ffn_s)
    taps = [jnp.concatenate([a, b], axis=0) for a, b in zip(taps_m, taps_s)]
    g_small = _ffn_conv_small(taps, ffn_conv_w, fbias, dff=DFF, tf=tf)
    tm_f = _row_tile(SEQ, 512)
    halo = jnp.stack([u_meta[r0 // SEQ, NM - 8:] if r0 % SEQ == 0 else u[r0 - 8:r0] for r0 in range(0, MB, tm_f)],
                     axis=0)
    x2 = _ffn_down(u, halo, g_small, ffn_conv_w, fbias, w_down.astype(BF16), x1f, ln2_g[None, :], ln2_b[None, :],
                   alpha, m_big=MB, dff=DFF, tm=tm_f, tk=tf, split=last)

    conv_ffn_p = jnp.stack([u[(n + 1) * SEQ - (kf - 1):(n + 1) * SEQ] for n in range(B)], axis=0)
    conv_ffn_new_s = jnp.concatenate([conv_ffn_s, u_samp], axis=1)[:, T:]

    k_p, v_p = _kv_states(p_main, n_b=B, seq=SEQ, nm=NM, n_h=H, dh=DH, k_off=k_off, v_off=v_off,
                          rows=_row_tile(SEQ, 512))
    lf_p = jnp.concatenate([lf_meta, lf_real], axis=1)
    k_sm = p_s[:, k_off:k_off + AW].reshape(NB, T, H, DH)
    v_sm = p_s[:, v_off:v_off + AW].reshape(NB, T, H, DH)
    lf_sm = logf[MB + MM:].reshape(NB, T, H)
    states_p = (k_p, v_p, lf_p, conv_ssm_p, h_real, conv_ffn_p)
    states_s = (k_sm, v_sm, lf_sm, conv_ssm_new_s, h_samp, conv_ffn_new_s)
    return x2, states_p, states_s


def kernel(x_prompt, x_sample, cache_k, cache_v, cache_logf, state_conv_ssm, state_ssm, state_conv_ffn, page_table, meta_tokens, ln_in_g, ln_in_b, w_in, b_f, w_att_out, ssm_conv_w, ssm_conv_b, dt_bias, a_log, d_skip, ssm_norm_w, w_ssm_out, w_o, ln1_g, ln1_b, w_up, ffn_conv_w, ffn_conv_b, w_down, ln2_g, ln2_b):
    B, SEQ, D = x_prompt.shape
    NB, T, _ = x_sample.shape
    depth = w_in.shape[0]
    NM = meta_tokens.shape[0]
    H, DH = cache_k.shape[3], cache_k.shape[4]
    SH, P, N = state_ssm.shape[2], state_ssm.shape[3], state_ssm.shape[4]
    DI = SH * P
    XBC = state_conv_ssm.shape[-1]
    G = (XBC - DI) // (2 * N)
    dims = dict(B=B, SEQ=SEQ, NM=NM, NB=NB, T=T, D=D, H=H, DH=DH, AW=H * DH, DI=DI, XBC=XBC, SH=SH,
                P=P, N=N, G=G, E=SH // G, DFF=w_down.shape[1], alpha=(2.0 * depth) ** 0.25)
    assert SEQ % CH == 0 and NM <= LANES and (B * SEQ) % NM == 0 and H + SH <= LANES

    xs_small = jnp.concatenate([jnp.broadcast_to(meta_tokens[None], (B, NM, D)).reshape(B * NM, D),
                                x_sample.reshape(NB * T, D)], axis=0)
    tr = _row_tile(B * SEQ, 512)
    hf, hb = _ln_in(x_prompt.reshape(B * SEQ, D), xs_small, ln_in_g[None, :], ln_in_b[None, :], tr)

    sp, ss = [], []
    for l in range(depth):
        lw = (w_in[l], b_f[l], w_att_out[l], ssm_conv_w[l], ssm_conv_b[l], dt_bias[l], a_log[l], d_skip[l],
              ssm_norm_w[l], w_ssm_out[l], w_o[l], ln1_g[l], ln1_b[l], w_up[l], ffn_conv_w[l], ffn_conv_b[l],
              w_down[l], ln2_g[l], ln2_b[l])
        st = (cache_k[l], cache_v[l], cache_logf[l], state_conv_ssm[l], state_ssm[l], state_conv_ffn[l])
        (hf, hb), st_p, st_s = _layer(hf, hb, lw, st, dims, page_table, l == depth - 1)
        sp.append(st_p)
        ss.append(st_s)

    stk = lambda lst, i: jnp.stack([s[i] for s in lst], axis=0)
    y_prompt = hf.reshape(B, SEQ, D)
    y_sample = hb[B * NM:].reshape(NB, T, D)
    return (y_prompt, y_sample, stk(sp, 0), stk(sp, 1), stk(sp, 2), stk(sp, 3), stk(sp, 4), stk(sp, 5),
            stk(ss, 0), stk(ss, 1), stk(ss, 2), stk(ss, 3), stk(ss, 4), stk(ss, 5))
```

```python
import functools

import jax
import jax.numpy as jnp
from jax import lax
from jax.experimental import pallas as pl
from jax.experimental.pallas import tpu as pltpu

F32 = jnp.float32
BF16 = jnp.bfloat16
LN_EPS = 1e-5
RMS_EPS = 1e-5
NEG = -1e30
LOG2E = 1.4426950408889634
LANES = 128
VMEM_LIMIT = 56 * 1024 * 1024


def _cp(*sem):
    return pltpu.CompilerParams(dimension_semantics=sem, vmem_limit_bytes=VMEM_LIMIT)


def _row_tile(m, target, mult=16):
    if m <= target:
        return m
    best = None
    for t in range(mult, target + 1, mult):
        if m % t == 0:
            best = t
    assert best is not None, (m, target)
    return best


def _dot(a, b):
    return jnp.dot(a, b, preferred_element_type=F32)


def _dot_nt(a, b):
    return lax.dot_general(a, b, (((1,), (1,)), ((), ())), preferred_element_type=F32)


def _dot_tn(a, b):
    return lax.dot_general(a, b, (((0,), (0,)), ((), ())), preferred_element_type=F32)


def _split3(x):
    hi = x.astype(BF16)
    r = x - hi.astype(F32)
    mid = r.astype(BF16)
    lo = (r - mid.astype(F32)).astype(BF16)
    return hi, mid, lo


def _sum01_left(m01, x):
    hi, mid, lo = _split3(x)
    return (_dot(m01, lo) + _dot(m01, mid)) + _dot(m01, hi)


def _sum01_right(x, m01):
    hi, mid, lo = _split3(x)
    return (_dot(lo, m01) + _dot(mid, m01)) + _dot(hi, m01)


def _softplus(x):
    return jnp.maximum(x, 0.0) + jnp.log1p(jnp.exp(-jnp.abs(x)))


def _silu(x):
    return x * (1.0 / (1.0 + jnp.exp(-x)))


def _sigmoid(x):
    return 1.0 / (1.0 + jnp.exp(-x))


def _ln_rows(x, g, b):
    mu = jnp.mean(x, -1, keepdims=True)
    xc = x - mu
    var = jnp.mean(xc * xc, -1, keepdims=True)
    return xc * lax.rsqrt(var + LN_EPS) * g + b


def _ln_in_kernel(xp_ref, xs_ref, g_ref, b_ref, hf_ref, hb_ref, *, n_big, m_small):
    i = pl.program_id(0)

    @pl.when(i < n_big)
    def _():
        y = _ln_rows(xp_ref[...], g_ref[...], b_ref[...])
        hf_ref[...] = y
        hb_ref[...] = y.astype(BF16)

    @pl.when(i == n_big)
    def _():
        y = _ln_rows(xs_ref[...], g_ref[...], b_ref[...])
        hf_ref[0:m_small, :] = y
        hb_ref[0:m_small, :] = y.astype(BF16)


def _ln_in(xp, xs, g, b, tr):
    mb, d = xp.shape
    ms = xs.shape[0]
    assert mb % tr == 0 and ms <= tr
    nb = mb // tr
    m = mb + ms
    return pl.pallas_call(
        functools.partial(_ln_in_kernel, n_big=nb, m_small=ms),
        grid=(nb + 1,),
        in_specs=[pl.BlockSpec((tr, d), lambda i: (jnp.minimum(i, nb - 1), 0)),
                  pl.BlockSpec((ms, d), lambda i: (0, 0)),
                  pl.BlockSpec((1, d), lambda i: (0, 0)),
                  pl.BlockSpec((1, d), lambda i: (0, 0))],
        out_specs=[pl.BlockSpec((tr, d), lambda i: (i, 0)),
                   pl.BlockSpec((tr, d), lambda i: (i, 0))],
        out_shape=[jax.ShapeDtypeStruct((m, d), F32), jax.ShapeDtypeStruct((m, d), BF16)],
        compiler_params=_cp("arbitrary"),
        name="ln_in",
    )(xp, xs, g, b)


def _mm_kernel(x_ref, w_ref, o_ref):
    o_ref[...] = _dot(x_ref[...], w_ref[...].astype(BF16)).astype(o_ref.dtype)


def _matmul(x, w, tm, tn, out_dtype=F32, name="matmul"):
    m, k = x.shape
    n = w.shape[1]
    assert m % tm == 0 and n % tn == 0
    return pl.pallas_call(
        _mm_kernel,
        grid=(m // tm, n // tn),
        in_specs=[pl.BlockSpec((tm, k), lambda i, j: (i, 0)),
                  pl.BlockSpec((k, tn), lambda i, j: (0, j))],
        out_specs=pl.BlockSpec((tm, tn), lambda i, j: (i, j)),
        out_shape=jax.ShapeDtypeStruct((m, n), out_dtype),
        compiler_params=_cp("parallel", "arbitrary"),
        name=name,
    )(x, w)


def _regroup_kernel(a_ref, sh_ref, *rest, n_src):
    del a_ref
    w_refs = rest[:n_src]
    o_ref = rest[n_src]
    tn = o_ref.shape[1]
    sh = sh_ref[pl.program_id(0)]
    wide = jnp.concatenate([r[...].astype(BF16) for r in w_refs], axis=1)
    lane = lax.broadcasted_iota(jnp.int32, (1, n_src * LANES), 1)
    wide = jnp.where((lane >= sh) & (lane < sh + tn), wide, jnp.zeros_like(wide))
    ri = lax.broadcasted_iota(jnp.int32, (n_src * LANES, tn), 0)
    ci = lax.broadcasted_iota(jnp.int32, (n_src * LANES, tn), 1)
    sel = jnp.where(ri == ci + sh, 1.0, 0.0).astype(BF16)
    o_ref[...] = _dot(wide, sel).astype(BF16)


def _regroup_w(w, layer, segments, tn):
    k, nin = w.shape[1], w.shape[2]
    n_src = tn // LANES + 1
    starts = []
    for src, width in segments:
        assert width % tn == 0
        starts += [src + c for c in range(0, width, tn)]
    last = (nin - 1) // LANES
    assert all(s // LANES + n_src - 1 <= last for s in starts)
    a_tab = jnp.asarray([s // LANES for s in starts], jnp.int32)
    sh_tab = jnp.asarray([s % LANES for s in starts], jnp.int32)
    nj = len(starts)
    w_specs = [pl.BlockSpec((None, k, LANES), functools.partial(lambda j, a, sh, t: (layer, 0, a[j] + t), t=t))
               for t in range(n_src)]
    grid_spec = pltpu.PrefetchScalarGridSpec(
        num_scalar_prefetch=2,
        grid=(nj,),
        in_specs=w_specs,
        out_specs=pl.BlockSpec((k, tn), lambda j, a, sh: (0, j)),
    )
    return pl.pallas_call(
        functools.partial(_regroup_kernel, n_src=n_src),
        grid_spec=grid_spec,
        out_shape=jax.ShapeDtypeStruct((k, nj * tn), BF16),
        compiler_params=_cp("arbitrary"),
        name="regroup_w",
    )(a_tab, sh_tab, *([w] * n_src))


def _small_proj_kernel(x_ref, w_ref, b_ref, o_ref, *, n_f):
    a = _dot(x_ref[...], w_ref[...].astype(BF16)) + b_ref[...]
    lane = lax.broadcasted_iota(jnp.int32, a.shape, 1)
    sp_pos = _softplus(a)
    ls = -_softplus(-a)
    o_ref[...] = jnp.where(lane < n_f, ls, sp_pos)


def _small_proj(x, w, b, tm, n_f):
    m, k = x.shape
    return pl.pallas_call(
        functools.partial(_small_proj_kernel, n_f=n_f),
        grid=(m // tm,),
        in_specs=[pl.BlockSpec((tm, k), lambda i: (i, 0)),
                  pl.BlockSpec((k, LANES), lambda i: (0, 0)),
                  pl.BlockSpec((1, LANES), lambda i: (0, 0))],
        out_specs=pl.BlockSpec((tm, LANES), lambda i: (i, 0)),
        out_shape=jax.ShapeDtypeStruct((m, LANES), F32),
        compiler_params=_cp("arbitrary"),
        name="small_proj",
    )(x, w, b)


def _cumsum_kernel(x_ref, o_ref, *, nblk):
    h = x_ref.shape[0]
    r = lax.broadcasted_iota(jnp.int32, (LANES, LANES), 0)
    c = lax.broadcasted_iota(jnp.int32, (LANES, LANES), 1)
    tri = jnp.where(r <= c, 1.0, 0.0).astype(BF16)
    carry = jnp.zeros((h, 1), F32)
    for j in range(nblk):
        blk = _sum01_right(x_ref[:, j * LANES:(j + 1) * LANES], tri) + carry
        o_ref[:, j * LANES:(j + 1) * LANES] = blk
        carry = blk[:, LANES - 1:LANES]


def _cumsum_lanes(x):
    n, h, length = x.shape
    return pl.pallas_call(
        functools.partial(_cumsum_kernel, nblk=length // LANES),
        grid=(n,),
        in_specs=[pl.BlockSpec((None, h, length), lambda i: (i, 0, 0))],
        out_specs=pl.BlockSpec((None, h, length), lambda i: (i, 0, 0)),
        out_shape=jax.ShapeDtypeStruct((n, h, length), F32),
        compiler_params=_cp("arbitrary"),
        name="logf_cumsum",
    )(x)


def _bias_lanes(f, key_side):
    hi, mid, lo = _split3(-f if key_side else f)
    lane = lax.broadcasted_iota(jnp.int32, (f.shape[0], LANES), 1)
    f0, o0 = (3, 0) if key_side else (0, 3)
    v = jnp.where(lane == f0, hi.astype(F32), jnp.where(lane == f0 + 1, mid.astype(F32), lo.astype(F32)))
    v = jnp.where((lane >= f0) & (lane < f0 + 3), v, jnp.where((lane >= o0) & (lane < o0 + 3), 1.0, 0.0))
    return v.astype(BF16)


def _fox_prompt_kernel(qr_ref, kr_ref, vr_ref, qm_ref, km_ref, vm_ref, fc_ref,
                       or_ref, om_ref, k_s, v_s, qa_s, m_s, l_s, acc_s, *, nm, bq, scale):
    h = pl.program_id(1)
    seq, dh = qr_ref.shape
    pad = LANES - nm
    nq = seq // bq
    lpx = LANES + seq

    hsel = lax.broadcasted_iota(jnp.int32, (1, fc_ref.shape[1]), 1) == h

    def fcol(start, size):
        blk = fc_ref[pl.ds(start, size), :]
        return jnp.sum(jnp.where(hsel, blk, 0.0), axis=1, keepdims=True) * LOG2E

    k_s[0:pad, 0:dh] = jnp.zeros((pad, dh), BF16)
    v_s[0:pad, :] = jnp.zeros((pad, dh), BF16)
    k_s[pad:LANES, 0:dh] = km_ref[...].astype(BF16)
    v_s[pad:LANES, :] = vm_ref[...].astype(BF16)
    k_s[0:LANES, dh:] = _bias_lanes(fcol(0, LANES), True)
    cb = 512 if seq % 512 == 0 else LANES
    for r0 in range(0, seq, cb):
        k_s[LANES + r0:LANES + r0 + cb, 0:dh] = kr_ref[r0:r0 + cb, :].astype(BF16)
        k_s[LANES + r0:LANES + r0 + cb, dh:] = _bias_lanes(fcol(LANES + r0, cb), True)
        v_s[LANES + r0:LANES + r0 + cb, :] = vr_ref[r0:r0 + cb, :].astype(BF16)

    qm = jnp.concatenate([(qm_ref[...] * scale).astype(BF16), _bias_lanes(fcol(pad, nm), False)], axis=1)
    rm = lax.broadcasted_iota(jnp.int32, (nm, LANES), 0)
    cm = lax.broadcasted_iota(jnp.int32, (nm, LANES), 1)
    sm = jnp.where((cm >= pad) & (cm - pad <= rm), _dot_nt(qm, k_s[0:LANES, :]), NEG)
    pm = jnp.exp2(sm - jnp.max(sm, axis=1, keepdims=True))
    om = _dot(pm.astype(BF16), v_s[0:LANES, :]) / jnp.sum(pm, axis=1, keepdims=True)
    om_ref[...] = om.astype(om_ref.dtype)

    rs = min(bq, 256)

    def update(r0, start, size, mask):
        s = _dot_nt(qa_s[r0:r0 + rs, :], k_s[pl.ds(start, size), :])
        if mask is not None:
            s = jnp.where(mask, s, NEG)
        m_old = m_s[r0:r0 + rs, :]
        m_new = jnp.maximum(m_old, jnp.max(s, axis=1, keepdims=True))
        alpha = jnp.exp2(m_old - m_new)
        nl = size // LANES
        p = jnp.exp2(s - (m_new if nl == 1 else jnp.concatenate([m_new] * nl, axis=1)))
        ps = p[:, 0:LANES]
        for u in range(1, nl):
            ps = ps + p[:, u * LANES:(u + 1) * LANES]
        l_s[r0:r0 + rs, :] = alpha * l_s[r0:r0 + rs, :] + ps
        acc_s[r0:r0 + rs, :] = alpha * acc_s[r0:r0 + rs, :] + _dot(p.astype(BF16), v_s[pl.ds(start, size), :])
        m_s[r0:r0 + rs, :] = m_new

    meta_cols = lax.broadcasted_iota(jnp.int32, (rs, LANES), 1) >= pad

    def qblock(i, _):
        q0 = pl.multiple_of(i * bq, bq)
        qa_s[:, 0:dh] = (qr_ref[pl.ds(q0, bq), :] * scale).astype(BF16)
        qa_s[:, dh:] = _bias_lanes(fcol(LANES + q0, bq), False)
        m_s[...] = jnp.full(m_s.shape, NEG, F32)
        l_s[...] = jnp.zeros(l_s.shape, F32)
        acc_s[...] = jnp.zeros(acc_s.shape, F32)
        for r0 in range(0, bq, rs):
            update(r0, 0, LANES, meta_cols)

        def kblock(j, c):
            for r0 in range(0, bq, rs):
                update(r0, pl.multiple_of(LANES + j * bq, LANES), bq, None)
            return c

        lax.fori_loop(0, i, kblock, 0)
        d0 = pl.multiple_of(LANES + q0, LANES)
        for r0 in range(0, bq, rs):
            size = r0 + rs
            rr = lax.broadcasted_iota(jnp.int32, (rs, size), 0)
            cc = lax.broadcasted_iota(jnp.int32, (rs, size), 1)
            update(r0, d0, size, cc <= rr + r0)
        l = jnp.sum(l_s[...], axis=1, keepdims=True)
        or_ref[pl.ds(q0, bq), :] = (acc_s[...] / l).astype(or_ref.dtype)
        return 0

    lax.fori_loop(0, nq, qblock, 0)


def _fox_prompt(p_main, f_c, *, n_b, seq, nm, n_h, dh, q_off, k_off, v_off, aw):
    bq = next(b for b in (1024, 512, 256, LANES) if seq % b == 0)
    assert dh == LANES
    qb, kb, vb = q_off // dh, k_off // dh, v_off // dh
    mrow = (n_b * seq) // nm
    lpx = LANES + seq
    real = lambda cb: pl.BlockSpec((seq, dh), lambda n, h: (n, cb + h))
    meta = lambda cb: pl.BlockSpec((nm, dh), lambda n, h: (mrow + n, cb + h))
    return pl.pallas_call(
        functools.partial(_fox_prompt_kernel, nm=nm, bq=bq, scale=dh ** -0.5 * LOG2E),
        grid=(n_b, n_h),
        in_specs=[real(qb), real(kb), real(vb), meta(qb), meta(kb), meta(vb),
                  pl.BlockSpec((None, lpx, n_h), lambda n, h: (n, 0, 0))],
        out_specs=[pl.BlockSpec((seq, dh), lambda n, h: (n, h)),
                   pl.BlockSpec((nm, dh), lambda n, h: (n, h))],
        out_shape=[jax.ShapeDtypeStruct((p_main.shape[0], aw), BF16),
                   jax.ShapeDtypeStruct((n_b * nm, aw), BF16)],
        scratch_shapes=[pltpu.VMEM((lpx, 2 * dh), BF16), pltpu.VMEM((lpx, dh), BF16),
                        pltpu.VMEM((bq, 2 * dh), BF16), pltpu.VMEM((bq, LANES), F32),
                        pltpu.VMEM((bq, LANES), F32), pltpu.VMEM((bq, dh), F32)],
        compiler_params=_cp("parallel", "arbitrary"),
        name="fox_prompt",
    )(p_main, p_main, p_main, p_main, p_main, p_main, f_c)


def _fox_sample_kernel(pt_ref, q_ref, kn_ref, vn_ref, lfn_ref, *rest, n_h, pps):
    k_refs = rest[0:pps]
    v_refs = rest[pps:2 * pps]
    lf_refs = rest[2 * pps:3 * pps]
    o_ref = rest[3 * pps]
    q_s, mb_s, m_s, l_s, acc_s, car_s, kn_s, vn_s, m12_s = rest[3 * pps + 1:]
    g = pl.program_id(1)
    ng = pl.num_programs(1)
    nq, dh = q_ref.shape
    nph = lf_refs[0].shape[0]

    def suffix_rows(lf):
        r = lf.shape[0]
        hi, mid, lo = _split3(lf)
        res = _dot(jnp.concatenate([lo, mid, hi], axis=0), m12_s[...])
        res = (res[0:r] + res[r:2 * r]) + res[2 * r:3 * r]
        return res[:, 0:LANES], res[:, LANES:2 * LANES]

    def update(s, vb):
        m_old = m_s[...]
        m_new = jnp.maximum(m_old, jnp.max(s, axis=1, keepdims=True))
        alpha = jnp.exp2(m_old - m_new)
        p = jnp.exp2(s - m_new)
        l_s[...] = alpha * l_s[...] + jnp.sum(p, axis=1, keepdims=True)
        acc_s[...] = alpha * acc_s[...] + _dot(p.astype(BF16), vb)
        m_s[...] = m_new

    @pl.when(g == 0)
    def _():
        ci = lax.broadcasted_iota(jnp.int32, (LANES, LANES), 0)
        cj = lax.broadcasted_iota(jnp.int32, (LANES, LANES), 1)
        same_h = (ci % n_h) == (cj % n_h)
        m12_s[...] = jnp.concatenate([jnp.where(same_h & (ci // n_h > cj // n_h), 1.0, 0.0),
                                      jnp.where(same_h, 1.0, 0.0)], axis=1).astype(BF16)
        q_s[...] = (q_ref[...] * (dh ** -0.5 * LOG2E)).astype(BF16)
        kn_s[...] = jnp.zeros(kn_s.shape, BF16)
        vn_s[...] = jnp.zeros(vn_s.shape, BF16)
        kn_s[0:nq, :] = kn_ref[...].astype(BF16)
        vn_s[0:nq, :] = vn_ref[...].astype(BF16)
        within, total = suffix_rows(lfn_ref[...])
        sfx = within[0:1, :] * LOG2E
        ri = lax.broadcasted_iota(jnp.int32, (nq, LANES), 0)
        li = lax.broadcasted_iota(jnp.int32, (nq, LANES), 1)
        rowc = -jnp.sum(jnp.where(ri == li, sfx, 0.0), axis=1, keepdims=True)
        head_ok = (ri % n_h) == (li % n_h)
        mb_s[...] = jnp.where(head_ok, rowc, NEG)
        m_s[...] = jnp.full(m_s.shape, NEG, F32)
        l_s[...] = jnp.zeros(l_s.shape, F32)
        acc_s[...] = jnp.zeros(acc_s.shape, F32)
        s = _dot_nt(q_s[...], kn_s[...]) + sfx + mb_s[...]
        s = jnp.where(li // n_h <= ri // n_h, s, NEG)
        update(s, vn_s[...])
        car_s[...] = total[0:1, :]

    q = q_s[...]
    mb = mb_s[...]
    prow = k_refs[0].shape[0]
    rows = lax.broadcasted_iota(jnp.int32, (nph, LANES), 0)
    within_all, total_all = suffix_rows(jnp.concatenate([lf_refs[j][...] for j in range(pps)], axis=0))
    car = car_s[...]
    s_parts = []
    for j in range(pps):
        total = total_all[j * nph:(j + 1) * nph]
        later = jnp.zeros((nph, LANES), F32)
        for r in range(1, nph):
            later = later + jnp.where(rows < r, total[r:r + 1, :], 0.0)
        bias = (within_all[j * nph:(j + 1) * nph] + later + car) * LOG2E
        car = car + jnp.sum(total, axis=0, keepdims=True)
        s = _dot_nt(q, k_refs[j][...].astype(BF16))
        s_parts += [s[:, r * LANES:(r + 1) * LANES] + (bias[r:r + 1, :] + mb) for r in range(nph)]
    car_s[...] = car
    s = jnp.concatenate(s_parts, axis=1)
    m_old = m_s[...]
    m_new = jnp.maximum(m_old, jnp.max(s, axis=1, keepdims=True))
    alpha = jnp.exp2(m_old - m_new)
    p = jnp.exp2(s - m_new)
    l_s[...] = alpha * l_s[...] + jnp.sum(p, axis=1, keepdims=True)
    pb = p.astype(BF16)
    pv = _dot(pb[:, 0:prow], v_refs[0][...].astype(BF16))
    for j in range(1, pps):
        pv = pv + _dot(pb[:, j * prow:(j + 1) * prow], v_refs[j][...].astype(BF16))
    acc_s[...] = alpha * acc_s[...] + pv
    m_s[...] = m_new

    @pl.when(g == ng - 1)
    def _():
        o_ref[...] = acc_s[...] / l_s[...]


def _fox_sample(q, kn, vn, lfn, ck, cv, clf, page_table, *, n_h, pps):
    nb, nq, dh = q.shape
    npg = page_table.shape[1]
    assert npg % pps == 0 and nq <= LANES and LANES % n_h == 0
    prow = ck.shape[1]
    nph = clf.shape[1]
    seq_spec = lambda r: pl.BlockSpec((None, r, dh), lambda b, g, pt: (b, 0, 0))

    def page_spec(rows, width, j):
        return pl.BlockSpec((None, rows, width), lambda b, g, pt: (pt[b, npg - 1 - (g * pps + j)], 0, 0))

    in_specs = [seq_spec(nq), seq_spec(nq), seq_spec(nq), pl.BlockSpec((None, 16, LANES), lambda b, g, pt: (b, 0, 0))]
    in_specs += [page_spec(prow, dh, j) for j in range(pps)]
    in_specs += [page_spec(prow, dh, j) for j in range(pps)]
    in_specs += [page_spec(nph, LANES, j) for j in range(pps)]
    grid_spec = pltpu.PrefetchScalarGridSpec(
        num_scalar_prefetch=1,
        grid=(nb, npg // pps),
        in_specs=in_specs,
        out_specs=pl.BlockSpec((None, nq, dh), lambda b, g, pt: (b, 0, 0)),
        scratch_shapes=[pltpu.VMEM((nq, dh), BF16), pltpu.VMEM((nq, LANES), F32),
                        pltpu.VMEM((nq, 1), F32), pltpu.VMEM((nq, 1), F32), pltpu.VMEM((nq, dh), F32),
                        pltpu.VMEM((1, LANES), F32),
                        pltpu.VMEM((LANES, dh), BF16), pltpu.VMEM((LANES, dh), BF16),
                        pltpu.VMEM((LANES, 2 * LANES), BF16)],
    )
    return pl.pallas_call(
        functools.partial(_fox_sample_kernel, n_h=n_h, pps=pps),
        grid_spec=grid_spec,
        out_shape=jax.ShapeDtypeStruct((nb, nq, dh), F32),
        compiler_params=_cp("parallel", "arbitrary"),
        name="fox_sample",
    )(page_table, q, kn, vn, lfn, *([ck] * pps), *([cv] * pps), *([clf] * pps))


CH = 128


def _ssd_kernel(xs_ref, b_ref, c_ref, dtc_ref, dtr_ref, alc_ref, alr_ref, dsk_ref,
                wx_ref, wb_ref, wc_ref, bx_ref, bb_ref, bc_ref, px_ref, pb_ref, pc_ref, h0_ref,
                y_ref, hT_ref, ex_s, eb_s, ec_s, h_s, *, t_valid, n_sub, n_e, p_dim):
    c = pl.program_id(2)
    nc = pl.num_programs(2)
    kw = wx_ref.shape[0]
    base = 8
    rows = CH * n_sub
    rows_in = t_valid if t_valid < CH else rows

    @pl.when(c == 0)
    def _():
        for e_s, p_ref in ((ex_s, px_ref), (eb_s, pb_ref), (ec_s, pc_ref)):
            e_s[...] = jnp.zeros(e_s.shape, F32)
            e_s[base - (kw - 1):base, :] = p_ref[...]
        h_s[...] = h0_ref[...].reshape(h_s.shape)

    lq = CH if t_valid == CH else 16 * ((t_valid + 15) // 16)
    rows_c = rows if t_valid == CH else lq

    def conv_silu(e_s, u_ref, w_ref, bias_ref):
        e_s[base:base + rows_in, :] = u_ref[...]
        out = bias_ref[...]
        for j in range(kw):
            out = out + e_s[base - (kw - 1) + j:base - (kw - 1) + j + rows_c, :] * w_ref[j:j + 1, :]
        return _silu(out)

    xc_all = conv_silu(ex_s, xs_ref, wx_ref, bx_ref)
    bm_all = conv_silu(eb_s, b_ref, wb_ref, bb_ref)
    cm_all = conv_silu(ec_s, c_ref, wc_ref, bc_ref)
    if t_valid == CH:
        for e_s in (ex_s, eb_s, ec_s):
            e_s[0:base, :] = e_s[rows:rows + base, :]
    else:
        valid = lax.broadcasted_iota(jnp.int32, (lq, 1), 0) < t_valid
        xc_all = jnp.concatenate([jnp.where(valid, xc_all, 0.0), jnp.zeros((CH - lq, xc_all.shape[1]), F32)], axis=0)
        bm_all = jnp.concatenate([jnp.where(valid, bm_all, 0.0), jnp.zeros((CH - lq, bm_all.shape[1]), F32)], axis=0)

    neg_a_c = -jnp.exp(alc_ref[...])
    neg_a_r = -jnp.exp(alr_ref[...])
    ri = lax.broadcasted_iota(jnp.int32, (CH, CH), 0)
    li = lax.broadcasted_iota(jnp.int32, (CH, CH), 1)
    causal = li <= ri
    tril = jnp.where(causal, 1.0, 0.0).astype(BF16)
    triu = jnp.where(ri <= li, 1.0, 0.0).astype(BF16)
    lane2 = lax.broadcasted_iota(jnp.int32, (CH, 2 * p_dim), 1) < p_dim
    row2 = lax.broadcasted_iota(jnp.int32, (2 * p_dim, 1), 0) < p_dim
    causal_l = lax.broadcasted_iota(jnp.int32, (lq, CH), 1) <= lax.broadcasted_iota(jnp.int32, (lq, CH), 0)
    lane2_l = lax.broadcasted_iota(jnp.int32, (lq, 2 * p_dim), 1) < p_dim

    for sub in range(n_sub):
        r0 = sub * CH
        xc = xc_all[r0:r0 + CH]
        dtc = dtc_ref[r0:r0 + CH, :]
        dtr = dtr_ref[:, r0:r0 + CH]
        cum_c = _sum01_left(tril, dtc * neg_a_c) * LOG2E
        cum_r = _sum01_right(dtr * neg_a_r, triu) * LOG2E
        end_c = cum_c[CH - 1:CH, :]
        dd_c = jnp.exp2(end_c - cum_c) * dtc
        ecum_c = jnp.exp2(cum_c)
        cdec_r = jnp.exp2(cum_r[:, CH - 1:CH])

        xb = xc.astype(BF16)
        bmb = bm_all[r0:r0 + CH].astype(BF16)
        cmb = cm_all[r0:r0 + lq].astype(BF16)
        cb = _dot_nt(cmb, bmb)
        cum_l, ecum_l = cum_c[0:lq], ecum_c[0:lq]
        ys = []
        for pr in range(n_e // 2):
            e0, e1 = 2 * pr, 2 * pr + 1
            xp = xb[:, e0 * p_dim:(e1 + 1) * p_dim]
            xpf = xc[:, e0 * p_dim:(e1 + 1) * p_dim]
            yd = []
            for e in (e0, e1):
                seg = cum_l[:, e:e + 1] - cum_r[e:e + 1, :]
                dec = jnp.exp2(jnp.where(causal_l, seg, -jnp.inf))
                w = (cb * dec * dtr[e:e + 1, :]).astype(BF16)
                yd.append(_dot(w, xp))
            y_diag = jnp.where(lane2_l, yd[0], yd[1])
            hp = h_s[e0 * p_dim:(e1 + 1) * p_dim, :]
            y_off = _dot_nt(cmb, hp.astype(BF16)) * jnp.where(lane2_l, ecum_l[:, e0:e0 + 1], ecum_l[:, e1:e1 + 1])
            ys.append(y_diag + y_off + xpf[0:lq] * dsk_ref[:, e0 * p_dim:(e1 + 1) * p_dim])
            xw = (xpf * jnp.where(lane2, dd_c[:, e0:e0 + 1], dd_c[:, e1:e1 + 1])).astype(BF16)
            st = _dot_tn(xw, bmb)
            cd = jnp.where(row2, cdec_r[e0:e0 + 1, :], cdec_r[e1:e1 + 1, :])
            h_s[e0 * p_dim:(e1 + 1) * p_dim, :] = hp * cd + st
        y = jnp.concatenate(ys, axis=1)
        if t_valid < CH:
            y_ref[...] = y[0:t_valid, :]
        else:
            y_ref[r0:r0 + CH, :] = y

    @pl.when(c == nc - 1)
    def _():
        hT_ref[...] = h_s[...].reshape(hT_ref.shape)


def _ssd(xsrc, col_x, col_b, col_c, t_valid, n_seq, n_chunks, dtc, dtr, alc, alr, dsk,
         conv_w, conv_b, prev, h0, *, n_g, n_e, p_dim, n_state, d_inner, y_rows=None):
    gw = n_e * p_dim
    kw = conv_w.shape[0]
    three_d = xsrc.ndim == 3
    assert three_d or t_valid == CH
    n_sub = 2 if (not three_d and n_chunks % 2 == 0) else 1
    n_steps = n_chunks // n_sub
    rows = CH * n_sub
    row_of = lambda n, c: n * n_steps + c

    def src_spec(width, colblk):
        if three_d:
            return pl.BlockSpec((None, t_valid, width), lambda n, g, c: (n, 0, colblk(g)))
        return pl.BlockSpec((rows, width), lambda n, g, c: (row_of(n, c), colblk(g)))

    cx = lambda g: col_x // gw + g
    cbk = lambda g: col_b // n_state + g
    cck = lambda g: col_c // n_state + g
    wx = lambda g: g
    wb = lambda g: d_inner // n_state + g
    wc = lambda g: (d_inner + n_g * n_state) // n_state + g

    if three_d:
        dtc_spec = pl.BlockSpec((None, None, CH, n_e), lambda n, g, c: (g, n, 0, 0))
        dtr_spec = pl.BlockSpec((None, n_e, CH), lambda n, g, c: (n, g, 0))
        y_spec = pl.BlockSpec((None, t_valid, gw), lambda n, g, c: (n, 0, g))
        y_shape = jax.ShapeDtypeStruct((n_seq, t_valid, d_inner), F32)
    else:
        dtc_spec = pl.BlockSpec((None, rows, n_e), lambda n, g, c: (g, row_of(n, c), 0))
        dtr_spec = pl.BlockSpec((n_e, rows), lambda n, g, c: (g, row_of(n, c)))
        y_spec = pl.BlockSpec((rows, gw), lambda n, g, c: (row_of(n, c), g))
        y_shape = jax.ShapeDtypeStruct((y_rows or n_seq * n_chunks * CH, d_inner), F32)

    in_specs = [
        src_spec(gw, cx), src_spec(n_state, cbk), src_spec(n_state, cck),
        dtc_spec, dtr_spec,
        pl.BlockSpec((None, 1, n_e), lambda n, g, c: (g, 0, 0)),
        pl.BlockSpec((n_e, 1), lambda n, g, c: (g, 0)),
        pl.BlockSpec((1, gw), lambda n, g, c: (0, g)),
        pl.BlockSpec((kw, gw), lambda n, g, c: (0, wx(g))),
        pl.BlockSpec((kw, n_state), lambda n, g, c: (0, wb(g))),
        pl.BlockSpec((kw, n_state), lambda n, g, c: (0, wc(g))),
        pl.BlockSpec((1, gw), lambda n, g, c: (0, wx(g))),
        pl.BlockSpec((1, n_state), lambda n, g, c: (0, wb(g))),
        pl.BlockSpec((1, n_state), lambda n, g, c: (0, wc(g))),
        pl.BlockSpec((None, kw - 1, gw), lambda n, g, c: (n, 0, wx(g))),
        pl.BlockSpec((None, kw - 1, n_state), lambda n, g, c: (n, 0, wb(g))),
        pl.BlockSpec((None, kw - 1, n_state), lambda n, g, c: (n, 0, wc(g))),
        pl.BlockSpec((None, n_e, p_dim, n_state), lambda n, g, c: (n, g, 0, 0)),
    ]
    out_specs = [y_spec, pl.BlockSpec((None, n_e, p_dim, n_state), lambda n, g, c: (n, g, 0, 0))]
    out_shape = [y_shape, jax.ShapeDtypeStruct((n_seq, n_g * n_e, p_dim, n_state), F32)]
    return pl.pallas_call(
        functools.partial(_ssd_kernel, t_valid=t_valid, n_sub=n_sub, n_e=n_e, p_dim=p_dim),
        grid=(n_seq, n_g, n_steps),
        in_specs=in_specs, out_specs=out_specs, out_shape=out_shape,
        scratch_shapes=[pltpu.VMEM((rows + 8, gw), F32), pltpu.VMEM((rows + 8, n_state), F32),
                        pltpu.VMEM((rows + 8, n_state), F32), pltpu.VMEM((n_e * p_dim, n_state), F32)],
        compiler_params=_cp("parallel", "parallel", "arbitrary"),
        name="ssd",
    )(xsrc, xsrc, xsrc, dtc, dtr, alc, alr, dsk, conv_w, conv_w, conv_w, conv_b, conv_b, conv_b,
      prev, prev, prev, h0)


def _gated_norm_kernel(y_ref, z_ref, w_ref, o_ref):
    g = y_ref[...] * _silu(z_ref[...])
    ms = jnp.mean(g * g, -1, keepdims=True)
    o_ref[...] = (g * lax.rsqrt(ms + RMS_EPS) * w_ref[...]).astype(o_ref.dtype)


def _gated_norm(y, p_main, z_off, w, tm):
    m, di = y.shape
    assert z_off % di == 0
    zb = z_off // di
    return pl.pallas_call(
        _gated_norm_kernel,
        grid=(m // tm,),
        in_specs=[pl.BlockSpec((tm, di), lambda i: (i, 0)),
                  pl.BlockSpec((tm, di), lambda i: (i, zb)),
                  pl.BlockSpec((1, di), lambda i: (0, 0))],
        out_specs=pl.BlockSpec((tm, di), lambda i: (i, 0)),
        out_shape=jax.ShapeDtypeStruct((m, di), BF16),
        compiler_params=_cp("arbitrary"),
        name="gated_rmsnorm",
    )(y, p_main, w)


def _proj_gate_kernel(x_ref, w_ref, g_ref, *rest, add):
    if add:
        a_ref, o_ref = rest
    else:
        (o_ref,) = rest
    v = _sigmoid(g_ref[...]) * _dot(x_ref[...], w_ref[...])
    if add:
        v = v + a_ref[...]
    o_ref[...] = v.astype(o_ref.dtype)


def _proj_gate(x, w, p_main, g_off, addend, tm, tn, out_dtype):
    m, k = x.shape
    n = w.shape[1]
    assert g_off % tn == 0 and n % tn == 0
    gb = g_off // tn
    in_specs = [pl.BlockSpec((tm, k), lambda i, j: (i, 0)),
                pl.BlockSpec((k, tn), lambda i, j: (0, j)),
                pl.BlockSpec((tm, tn), lambda i, j: (i, gb + j))]
    args = [x, w, p_main]
    if addend is not None:
        in_specs.append(pl.BlockSpec((tm, tn), lambda i, j: (i, j)))
        args.append(addend)
    return pl.pallas_call(
        functools.partial(_proj_gate_kernel, add=addend is not None),
        grid=(m // tm, n // tn),
        in_specs=in_specs,
        out_specs=pl.BlockSpec((tm, tn), lambda i, j: (i, j)),
        out_shape=jax.ShapeDtypeStruct((m, n), out_dtype),
        compiler_params=_cp("parallel", "arbitrary"),
        name="proj_gate",
    )(*args)


def _mm_res_ln_kernel(x_ref, w_ref, r_ref, g_ref, b_ref, of_ref, ob_ref, acc_s, *, alpha):
    k = pl.program_id(1)

    @pl.when(k == 0)
    def _():
        acc_s[...] = alpha * r_ref[...]

    acc_s[...] += _dot(x_ref[...], w_ref[...].astype(BF16))

    @pl.when(k == pl.num_programs(1) - 1)
    def _():
        y = _ln_rows(acc_s[...], g_ref[...], b_ref[...])
        of_ref[...] = y
        ob_ref[...] = y.astype(BF16)


def _mm_res_ln(x, w, res, g, b, alpha, tm, tk):
    m, k = x.shape
    n = w.shape[1]
    assert m % tm == 0 and k % tk == 0
    return pl.pallas_call(
        functools.partial(_mm_res_ln_kernel, alpha=alpha),
        grid=(m // tm, k // tk),
        in_specs=[pl.BlockSpec((tm, tk), lambda i, kk: (i, kk)),
                  pl.BlockSpec((tk, n), lambda i, kk: (kk, 0)),
                  pl.BlockSpec((tm, n), lambda i, kk: (i, 0)),
                  pl.BlockSpec((1, n), lambda i, kk: (0, 0)),
                  pl.BlockSpec((1, n), lambda i, kk: (0, 0))],
        out_specs=[pl.BlockSpec((tm, n), lambda i, kk: (i, 0)), pl.BlockSpec((tm, n), lambda i, kk: (i, 0))],
        out_shape=[jax.ShapeDtypeStruct((m, n), F32), jax.ShapeDtypeStruct((m, n), BF16)],
        scratch_shapes=[pltpu.VMEM((tm, n), F32)],
        compiler_params=_cp("parallel", "arbitrary"),
        name="mm_res_ln",
    )(x, w, res, g, b)


def _put_rows_kernel(dst_ref, src_ref, o_ref):
    del dst_ref
    o_ref[...] = src_ref[...]


def _put_rows(dst, src, row0, rb):
    n, w = src.shape
    assert row0 % rb == 0 and n % rb == 0 and row0 + n <= dst.shape[0]
    return pl.pallas_call(
        _put_rows_kernel,
        grid=(n // rb,),
        in_specs=[pl.BlockSpec(memory_space=pl.ANY), pl.BlockSpec((rb, w), lambda i: (i, 0))],
        out_specs=pl.BlockSpec((rb, w), lambda i: (row0 // rb + i, 0)),
        out_shape=jax.ShapeDtypeStruct(dst.shape, dst.dtype),
        input_output_aliases={0: 0},
        compiler_params=_cp("arbitrary"),
        name="put_rows",
    )(dst, src)


def _kv_states_kernel(kc_ref, kp_ref, vc_ref, vp_ref, ko_ref, vo_ref, *, n_h, nm):
    rows, aw = kc_ref.shape
    dh = aw // n_h
    for c_ref, p_ref, o_ref in ((kc_ref, kp_ref, ko_ref), (vc_ref, vp_ref, vo_ref)):
        for h in range(n_h):
            o_ref[pl.ds(h, nm, stride=n_h), :] = p_ref[:, h * dh:(h + 1) * dh]
            o_ref[pl.ds(nm * n_h + h, rows - nm, stride=n_h), :] = c_ref[0:rows - nm, h * dh:(h + 1) * dh]


def _kv_states(p_main, *, n_b, seq, nm, n_h, dh, k_off, v_off, rows):
    aw = n_h * dh
    assert k_off % aw == 0 and v_off % aw == 0 and seq % rows == 0 and rows % nm == 0 and (n_b * seq) % nm == 0
    nsb = seq // rows
    mb = n_b * seq
    cur = lambda cb: pl.BlockSpec((rows, aw), lambda n, i: (n * nsb + jnp.minimum(i, nsb - 1), cb))
    prev = lambda cb: pl.BlockSpec(
        (nm, aw), lambda n, i: (jnp.where(i == 0, mb // nm + n, (n * seq + rows * i) // nm - 1), cb))
    out_spec = pl.BlockSpec((None, rows * n_h, dh), lambda n, i: (n, i, 0))
    shape = jax.ShapeDtypeStruct((n_b, (nm + seq) * n_h, dh), F32)
    k, v = pl.pallas_call(
        functools.partial(_kv_states_kernel, n_h=n_h, nm=nm),
        grid=(n_b, nsb + 1),
        in_specs=[cur(k_off // aw), prev(k_off // aw), cur(v_off // aw), prev(v_off // aw)],
        out_specs=[out_spec, out_spec], out_shape=[shape, shape],
        compiler_params=_cp("arbitrary", "arbitrary"),
        name="kv_states",
    )(p_main, p_main, p_main, p_main)
    return k.reshape(n_b, nm + seq, n_h, dh), v.reshape(n_b, nm + seq, n_h, dh)


def _ffn_conv_small_kernel(*refs, kw):
    ua = refs[0:kw]
    ub = refs[kw:2 * kw]
    wa_ref, wb_ref, ba_ref, bb_ref, o_ref = refs[2 * kw:]
    a = ba_ref[...]
    b = bb_ref[...]
    for j in range(kw):
        a = a + ua[j][...] * wa_ref[j:j + 1, :]
        b = b + ub[j][...] * wb_ref[j:j + 1, :]
    o_ref[...] = (_silu(a) * b).astype(o_ref.dtype)


def _ffn_conv_small(taps, conv_w, conv_b, *, dff, tf):
    kw = conv_w.shape[0]
    rows = taps[0].shape[0]
    nj = dff // tf
    a_specs = [pl.BlockSpec((rows, tf), lambda j: (0, j)) for _ in range(kw)]
    b_specs = [pl.BlockSpec((rows, tf), lambda j: (0, nj + j)) for _ in range(kw)]
    return pl.pallas_call(
        functools.partial(_ffn_conv_small_kernel, kw=kw),
        grid=(nj,),
        in_specs=a_specs + b_specs + [pl.BlockSpec((kw, tf), lambda j: (0, j)),
                                      pl.BlockSpec((kw, tf), lambda j: (0, nj + j)),
                                      pl.BlockSpec((1, tf), lambda j: (0, j)),
                                      pl.BlockSpec((1, tf), lambda j: (0, nj + j))],
        out_specs=pl.BlockSpec((rows, tf), lambda j: (0, j)),
        out_shape=jax.ShapeDtypeStruct((rows, dff), BF16),
        compiler_params=_cp("arbitrary"),
        name="ffn_conv_small",
    )(*taps, *taps, conv_w, conv_w, conv_b, conv_b)


def _ffn_down_kernel(ua_ref, ub_ref, ha_ref, hb_ref, wa_ref, wb_ref, ba_ref, bb_ref, gs_ref, w_ref, r_ref,
                     g_ref, b_ref, o0_ref, o1_ref, acc_s, ea_s, eb_s, *, alpha, n_big, m_small, split):
    i = pl.program_id(0)
    k = pl.program_id(1)
    kw = wa_ref.shape[0]
    tm = ua_ref.shape[0]
    base = 8

    @pl.when(k == 0)
    def _():
        acc_s[...] = alpha * r_ref[...]

    def conv(e_s, u_ref, h_ref, w_ref, bias_ref):
        e_s[0:base, :] = h_ref[...]
        e_s[base:base + tm, :] = u_ref[...]
        out = bias_ref[...]
        for j in range(kw):
            out = out + e_s[base - (kw - 1) + j:base - (kw - 1) + j + tm, :] * w_ref[j:j + 1, :]
        return out

    @pl.when(i < n_big)
    def _():
        a = conv(ea_s, ua_ref, ha_ref, wa_ref, ba_ref)
        b = conv(eb_s, ub_ref, hb_ref, wb_ref, bb_ref)
        acc_s[...] += _dot((_silu(a) * b).astype(BF16), w_ref[...])

    @pl.when(i == n_big)
    def _():
        acc_s[0:m_small, :] += _dot(gs_ref[...], w_ref[...])

    @pl.when(k == pl.num_programs(1) - 1)
    def _():
        y = _ln_rows(acc_s[...], g_ref[...], b_ref[...])
        if not split:
            o0_ref[...] = y
            o1_ref[...] = y.astype(BF16)
        else:
            @pl.when(i < n_big)
            def _():
                o0_ref[...] = y

            @pl.when(i == n_big)
            def _():
                o1_ref[...] = y[0:m_small, :]


def _ffn_down(u, halo, g_small, conv_w, conv_b, w, res, g, b, alpha, *, m_big, dff, tm, tk, split):
    m = u.shape[0]
    n = w.shape[1]
    kw = conv_w.shape[0]
    nj = dff // tk
    m_small = m - m_big
    assert m_big % tm == 0 and 0 < m_small <= tm and dff % tk == 0
    n_big = m_big // tm
    big = lambda i: jnp.minimum(i, n_big - 1)
    if split:
        out_specs = [pl.BlockSpec((tm, n), lambda i, k: (big(i), 0)), pl.BlockSpec((m_small, n), lambda i, k: (0, 0))]
        out_shape = [jax.ShapeDtypeStruct((m_big, n), F32), jax.ShapeDtypeStruct((m_small, n), F32)]
    else:
        out_specs = [pl.BlockSpec((tm, n), lambda i, k: (i, 0)), pl.BlockSpec((tm, n), lambda i, k: (i, 0))]
        out_shape = [jax.ShapeDtypeStruct((m, n), F32), jax.ShapeDtypeStruct((m, n), BF16)]
    return pl.pallas_call(
        functools.partial(_ffn_down_kernel, alpha=alpha, n_big=n_big, m_small=m_small, split=split),
        grid=(n_big + 1, nj),
        in_specs=[pl.BlockSpec((tm, tk), lambda i, k: (big(i), k)),
                  pl.BlockSpec((tm, tk), lambda i, k: (big(i), nj + k)),
                  pl.BlockSpec((None, 8, tk), lambda i, k: (big(i), 0, k)),
                  pl.BlockSpec((None, 8, tk), lambda i, k: (big(i), 0, nj + k)),
                  pl.BlockSpec((kw, tk), lambda i, k: (0, k)),
                  pl.BlockSpec((kw, tk), lambda i, k: (0, nj + k)),
                  pl.BlockSpec((1, tk), lambda i, k: (0, k)),
                  pl.BlockSpec((1, tk), lambda i, k: (0, nj + k)),
                  pl.BlockSpec((m_small, tk), lambda i, k: (0, k)),
                  pl.BlockSpec((tk, n), lambda i, k: (k, 0)),
                  pl.BlockSpec((tm, n), lambda i, k: (i, 0)),
                  pl.BlockSpec((1, n), lambda i, k: (0, 0)),
                  pl.BlockSpec((1, n), lambda i, k: (0, 0))],
        out_specs=out_specs, out_shape=out_shape,
        scratch_shapes=[pltpu.VMEM((tm, n), F32), pltpu.VMEM((tm + 8, tk), F32), pltpu.VMEM((tm + 8, tk), F32)],
        compiler_params=_cp("arbitrary", "arbitrary"),
        name="ffn_down",
    )(u, u, halo, halo, conv_w, conv_w, conv_b, conv_b, g_small, w, res, g, b)


def _shift_taps(u_seq, prev):
    n, t, c = u_seq.shape
    k1 = prev.shape[1]
    full = jnp.concatenate([prev.astype(u_seq.dtype), u_seq], axis=1)
    return [full[:, j:j + t].reshape(n * t, c) for j in range(k1 + 1)]


def _layer(hf, hb, lw, st, dims, page_table, last):
    (w_in, b_f, w_att_out, ssm_conv_w, ssm_conv_b, dt_bias, a_log, d_skip, ssm_norm_w,
     w_ssm_out, w_o, ln1_g, ln1_b, w_up, ffn_conv_w, ffn_conv_b, w_down, ln2_g, ln2_b) = lw
    cache_k, cache_v, cache_logf, conv_ssm_s, ssm_s, conv_ffn_s = st
    d = dims
    B, SEQ, NM, NB, T, D = d["B"], d["SEQ"], d["NM"], d["NB"], d["T"], d["D"]
    H, DH, AW, DI, XBC, SH = d["H"], d["DH"], d["AW"], d["DI"], d["XBC"], d["SH"]
    P, N, G, E, DFF, alpha = d["P"], d["N"], d["G"], d["E"], d["DFF"], d["alpha"]
    M = hf.shape[0]
    MB, MM, MS = B * SEQ, B * NM, NB * T
    tm = _row_tile(M, 1392)
    tm_s = _row_tile(M, 512)

    o = [0]
    for s in (AW, AW, AW, H, DI, XBC, SH, D):
        o.append(o[-1] + s)
    w_in3, layer = w_in
    wf = w_in3[layer, :, o[3]:o[3] + H]
    wdt = w_in3[layer, :, o[6]:o[6] + SH]
    segments = ((o[4], DI), (o[0], 3 * AW), (o[5], XBC), (o[7], 2 * D))
    z_off, q_off, k_off, v_off = 0, DI, DI + AW, DI + 2 * AW
    x_off = DI + 3 * AW
    ga_off = x_off + XBC
    gs_off = ga_off + D
    nmain = gs_off + D
    w_small = jnp.concatenate([wf, wdt, jnp.zeros((D, LANES - H - SH), F32)], axis=1)
    b_small = jnp.concatenate([b_f, dt_bias, jnp.zeros((LANES - H - SH,), F32)])[None, :]

    w_main = _regroup_w(w_in3, layer, segments, 512)
    p_main = _matmul(hb, w_main, tm, 512, F32, name="in_proj")
    s_small = _small_proj(hb, w_small, b_small, tm, H)
    logf = s_small[:, :H]
    dt = s_small[:, H:H + SH]

    pad = LANES - NM
    lf_real = logf[:MB].reshape(B, SEQ, H)
    lf_meta = logf[MB:MB + MM].reshape(B, NM, H)
    lf_ext = jnp.concatenate([jnp.zeros((B, pad, H), F32), lf_meta, lf_real], axis=1)
    f_c = _cumsum_lanes(lf_ext.transpose(0, 2, 1)).transpose(0, 2, 1)
    att, o_meta = _fox_prompt(p_main, f_c, n_b=B, seq=SEQ, nm=NM, n_h=H, dh=DH,
                              q_off=q_off, k_off=k_off, v_off=v_off, aw=AW)
    rb = 32
    assert MB % rb == 0 and (MM + MS) % rb == 0

    p_s = p_main[MB + MM:]
    q_s = p_s[:, q_off:q_off + AW].reshape(NB, T * H, DH)
    k_s = p_s[:, k_off:k_off + AW].reshape(NB, T * H, DH)
    v_s = p_s[:, v_off:v_off + AW].reshape(NB, T * H, DH)
    lf_s = logf[MB + MM:].reshape(NB, T * H)
    lfn = jnp.zeros((NB, 16, LANES), F32).at[:, 0, :T * H].set(lf_s)
    pool, PAGE = cache_k.shape[0], cache_k.shape[1]
    ck = cache_k.reshape(pool, PAGE * H, DH)
    cv = cache_v.reshape(pool, PAGE * H, DH)
    clf = cache_logf.astype(F32).reshape(pool, (PAGE * H) // LANES, LANES)
    npg = page_table.shape[1]
    pps = next(p for p in (8, 4, 2, 1) if npg % p == 0)
    o_s = _fox_sample(q_s, k_s, v_s, lfn, ck, cv, clf, page_table, n_h=H, pps=pps)
    o_samp = o_s.reshape(NB * T, AW).astype(BF16)
    att = _put_rows(att, jnp.concatenate([o_meta, o_samp], axis=0), MB, rb)

    dt_c = dt.reshape(M, G, E).transpose(1, 0, 2)
    dt_r = dt.T
    alc = a_log.reshape(G, 1, E)
    alr = a_log.reshape(SH, 1)
    dsk = jnp.repeat(d_skip, P)[None, :]
    cbias = ssm_conv_b[None, :]
    kw = ssm_conv_w.shape[0]
    ssd = functools.partial(_ssd, n_g=G, n_e=E, p_dim=P, n_state=N, d_inner=DI)
    col_b = x_off + DI
    col_c = col_b + G * N
    zeros_prev = jnp.zeros((B, kw - 1, XBC), F32)
    zeros_h = jnp.zeros((B, SH, P, N), F32)

    def short_dt(lo, n_seq, t):
        c3 = dt_c[:, lo:lo + n_seq * t].reshape(G, n_seq, t, E)
        r3 = dt_r[:, lo:lo + n_seq * t].reshape(SH, n_seq, t).transpose(1, 0, 2)
        return (jnp.pad(c3, ((0, 0), (0, 0), (0, CH - t), (0, 0))), jnp.pad(r3, ((0, 0), (0, 0), (0, CH - t))))

    p_m3 = p_main[MB:MB + MM].reshape(B, NM, nmain)
    dt_c_m, dt_r_m = short_dt(MB, B, NM)
    y_meta, h_meta = ssd(p_m3, x_off, col_b, col_c, NM, B, 1, dt_c_m, dt_r_m, alc, alr, dsk,
                         ssm_conv_w, cbias, zeros_prev, zeros_h)
    prev_real = p_m3[:, NM - (kw - 1):, x_off:x_off + XBC]
    y_all, h_real = ssd(p_main, x_off, col_b, col_c, CH, B, SEQ // CH, dt_c, dt_r, alc, alr, dsk,
                        ssm_conv_w, cbias, prev_real, h_meta, y_rows=M)
    p_s3 = p_s.reshape(NB, T, nmain)
    dt_c_s, dt_r_s = short_dt(MB + MM, NB, T)
    y_samp, h_samp = ssd(p_s3, x_off, col_b, col_c, T, NB, 1, dt_c_s, dt_r_s, alc, alr, dsk,
                         ssm_conv_w, cbias, conv_ssm_s, ssm_s)
    y_small = jnp.concatenate([y_meta.reshape(MM, DI), y_samp.reshape(MS, DI)], axis=0)
    y_all = _put_rows(y_all, y_small, MB, rb)
    conv_ssm_p = jnp.stack([p_main[(n + 1) * SEQ - (kw - 1):(n + 1) * SEQ, x_off:x_off + XBC] for n in range(B)],
                           axis=0)
    xbc_s = p_s3[:, :, x_off:x_off + XBC]
    conv_ssm_new_s = jnp.concatenate([conv_ssm_s, xbc_s], axis=1)[:, T:]

    yn = _gated_norm(y_all, p_main, z_off, ssm_norm_w[None, :], tm_s)

    tn_o = 512 if D % 512 == 0 else D
    tm_p = _row_tile(M, 928)
    m1 = _proj_gate(att, w_att_out.astype(BF16), p_main, ga_off, None, tm_p, tn_o, F32)
    merged = _proj_gate(yn, w_ssm_out.astype(BF16), p_main, gs_off, m1, tm_p, tn_o, BF16)
    x1f, x1b = _mm_res_ln(merged, w_o.astype(BF16), hf, ln1_g[None, :], ln1_b[None, :], alpha, tm_s,
                          D if D <= 2048 else 512)

    u = _matmul(x1b, w_up, tm, 512, F32, name="ffn_up")
    kf = ffn_conv_w.shape[0]
    fbias = ffn_conv_b[None, :]
    tf = 512 if DFF % 512 == 0 else DFF
    u_meta = u[MB:MB + MM].reshape(B, NM, 2 * DFF)
    u_samp = u[MB + MM:].reshape(NB, T, 2 * DFF)
    taps_m = _shift_taps(u_meta, jnp.zeros((B, kf - 1, 2 * DFF), F32))
    taps_s = _shift_taps(u_samp, conv_ffn_s)
    taps = [jnp.concatenate([a, b], axis=0) for a, b in zip(taps_m, taps_s)]
    g_small = _ffn_conv_small(taps, ffn_conv_w, fbias, dff=DFF, tf=tf)
    tm_f = _row_tile(SEQ, 512)
    halo = jnp.stack([u_meta[r0 // SEQ, NM - 8:] if r0 % SEQ == 0 else u[r0 - 8:r0] for r0 in range(0, MB, tm_f)],
                     axis=0)
    x2 = _ffn_down(u, halo, g_small, ffn_conv_w, fbias, w_down.astype(BF16), x1f, ln2_g[None, :], ln2_b[None, :],
                   alpha, m_big=MB, dff=DFF, tm=tm_f, tk=tf, split=last)

    conv_ffn_p = jnp.stack([u[(n + 1) * SEQ - (kf - 1):(n + 1) * SEQ] for n in range(B)], axis=0)
    conv_ffn_new_s = jnp.concatenate([conv_ffn_s, u_samp], axis=1)[:, T:]

    k_p, v_p = _kv_states(p_main, n_b=B, seq=SEQ, nm=NM, n_h=H, dh=DH, k_off=k_off, v_off=v_off,
                          rows=_row_tile(SEQ, 512))
    lf_p = jnp.concatenate([lf_meta, lf_real], axis=1)
    k_sm = p_s[:, k_off:k_off + AW].reshape(NB, T, H, DH)
    v_sm = p_s[:, v_off:v_off + AW].reshape(NB, T, H, DH)
    lf_sm = logf[MB + MM:].reshape(NB, T, H)
    states_p = (k_p, v_p, lf_p, conv_ssm_p, h_real, conv_ffn_p)
    states_s = (k_sm, v_sm, lf_sm, conv_ssm_new_s, h_samp, conv_ffn_new_s)
    return x2, states_p, states_s


def kernel(x_prompt, x_sample, cache_k, cache_v, cache_logf, state_conv_ssm, state_ssm, state_conv_ffn, page_table, meta_tokens, ln_in_g, ln_in_b, w_in, b_f, w_att_out, ssm_conv_w, ssm_conv_b, dt_bias, a_log, d_skip, ssm_norm_w, w_ssm_out, w_o, ln1_g, ln1_b, w_up, ffn_conv_w, ffn_conv_b, w_down, ln2_g, ln2_b):
    B, SEQ, D = x_prompt.shape
    NB, T, _ = x_sample.shape
    depth = w_in.shape[0]
    NM = meta_tokens.shape[0]
    H, DH = cache_k.shape[3], cache_k.shape[4]
    SH, P, N = state_ssm.shape[2], state_ssm.shape[3], state_ssm.shape[4]
    DI = SH * P
    XBC = state_conv_ssm.shape[-1]
    G = (XBC - DI) // (2 * N)
    dims = dict(B=B, SEQ=SEQ, NM=NM, NB=NB, T=T, D=D, H=H, DH=DH, AW=H * DH, DI=DI, XBC=XBC, SH=SH,
                P=P, N=N, G=G, E=SH // G, DFF=w_down.shape[1], alpha=(2.0 * depth) ** 0.25)
    assert SEQ % CH == 0 and NM <= LANES and (B * SEQ) % NM == 0 and H + SH <= LANES

    xs_small = jnp.concatenate([jnp.broadcast_to(meta_tokens[None], (B, NM, D)).reshape(B * NM, D),
                                x_sample.reshape(NB * T, D)], axis=0)
    tr = _row_tile(B * SEQ, 512)
    hf, hb = _ln_in(x_prompt.reshape(B * SEQ, D), xs_small, ln_in_g[None, :], ln_in_b[None, :], tr)

    sp, ss = [], []
    for l in range(depth):
        lw = ((w_in, l), b_f[l], w_att_out[l], ssm_conv_w[l], ssm_conv_b[l], dt_bias[l], a_log[l], d_skip[l],
              ssm_norm_w[l], w_ssm_out[l], w_o[l], ln1_g[l], ln1_b[l], w_up[l], ffn_conv_w[l], ffn_conv_b[l],
              w_down[l], ln2_g[l], ln2_b[l])
        st = (cache_k[l], cache_v[l], cache_logf[l], state_conv_ssm[l], state_ssm[l], state_conv_ffn[l])
        (hf, hb), st_p, st_s = _layer(hf, hb, lw, st, dims, page_table, l == depth - 1)
        sp.append(st_p)
        ss.append(st_s)

    stk = lambda lst, i: jnp.stack([s[i] for s in lst], axis=0)
    y_prompt = hf.reshape(B, SEQ, D)
    y_sample = hb[B * NM:].reshape(NB, T, D)
    return (y_prompt, y_sample, stk(sp, 0), stk(sp, 1), stk(sp, 2), stk(sp, 3), stk(sp, 4), stk(sp, 5),
            stk(ss, 0), stk(ss, 1), stk(ss, 2), stk(ss, 3), stk(ss, 4), stk(ss, 5))
```
